```python
import jax, jax.numpy as jnp
from jax import lax
import numpy as np

D_MODEL = 1024
BATCH = 2
SEQ = 8192
DEPTH = 2

PLE_DIM = 256
D_CONV = 512
N_CONV_GROUPS = 8
CONV_WIDTH = 31
N_RET_HEADS = 4
RET_HEAD_DIM = 128
D_RET = N_RET_HEADS * RET_HEAD_DIM
D_MIX = D_CONV + D_RET
D_IN_PROJ = 2 * D_CONV + 4 * D_RET
RET_CHUNK = 128
ROPE_BASE = 10000.0
D_FF = 2816
FFN_CONV_WIDTH = 3
EPS = 1e-6

kernel_name = "hybrid_conformer_retention_block"


def rms_norm(x, g):
    xf = x.astype(jnp.float32)
    y = xf * lax.rsqrt(jnp.mean(xf * xf, axis=-1, keepdims=True) + EPS)
    return (y * g.astype(jnp.float32)).astype(x.dtype)


def layer_norm(x, g, b):
    xf = x.astype(jnp.float32)
    mu = jnp.mean(xf, axis=-1, keepdims=True)
    var = jnp.mean(jnp.square(xf - mu), axis=-1, keepdims=True)
    y = (xf - mu) * lax.rsqrt(var + EPS)
    return (y * g.astype(jnp.float32) + b.astype(jnp.float32)).astype(x.dtype)


def causal_dwconv(x, w, b):
    k_width, ch = w.shape
    y = lax.conv_general_dilated(
        x, w[:, None, :].astype(x.dtype), window_strides=(1,),
        padding=[(k_width - 1, 0)], dimension_numbers=("NWC", "WIO", "NWC"),
        feature_group_count=ch)
    return y + b.astype(x.dtype)


def rotary_tables(positions):
    half = RET_HEAD_DIM // 2
    inv_freq = ROPE_BASE ** (-jnp.arange(half, dtype=jnp.float32) / half)
    ang = positions.astype(jnp.float32)[..., None] * inv_freq
    return jnp.cos(ang)[:, :, None, :], jnp.sin(ang)[:, :, None, :]


def apply_rotary(x, cos, sin):
    half = x.shape[-1] // 2
    x1, x2 = x[..., :half], x[..., half:]
    c, s = cos.astype(x.dtype), sin.astype(x.dtype)
    return jnp.concatenate([x1 * c - x2 * s, x2 * c + x1 * s], axis=-1)


def chunk_retention(q, k, v):
    bsz, seq, n_h, d_k = q.shape
    d_v = v.shape[-1]
    C = RET_CHUNK
    nc = seq // C
    log_g = jnp.log1p(-(2.0 ** (-5.0 - jnp.arange(n_h, dtype=jnp.float32))))
    idx = jnp.arange(C, dtype=jnp.float32)
    rel = idx[:, None] - idx[None, :]
    decay_in = jnp.where(rel >= 0, jnp.exp(log_g[:, None, None] * jnp.maximum(rel, 0.0)), 0.0).astype(q.dtype)
    zeta = jnp.exp(log_g[:, None] * (C - 1 - idx)).astype(q.dtype)
    xi = jnp.exp(log_g[:, None] * (idx + 1)).astype(q.dtype)
    chunk_decay = jnp.exp(log_g * C).astype(q.dtype)

    qc = q.reshape(bsz, nc, C, n_h, d_k)
    kc = k.reshape(bsz, nc, C, n_h, d_k)
    vc = v.reshape(bsz, nc, C, n_h, d_v)

    scores = jnp.einsum("bcnhd,bcmhd->bchnm", qc, kc) * decay_in
    inner = jnp.einsum("bchnm,bcmhe->bcnhe", scores, vc)

    kv = jnp.einsum("bcmhd,bcmhe,hm->cbhde", kc, vc, zeta)

    def step(state, kv_c):
        return state * chunk_decay[None, :, None, None] + kv_c, state

    _, state_prev = lax.scan(step, jnp.zeros_like(kv[0]), kv)
    cross = jnp.einsum("bcnhd,cbhde,hn->bcnhe", qc, state_prev, xi)
    return (inner + cross).reshape(bsz, seq, n_h, d_v)


def head_group_norm(o, g):
    of = o.astype(jnp.float32)
    mu = jnp.mean(of, axis=-1, keepdims=True)
    var = jnp.mean(jnp.square(of - mu), axis=-1, keepdims=True)
    y = ((of - mu) * lax.rsqrt(var + EPS)).reshape(o.shape[0], o.shape[1], -1)
    return (y * g.astype(jnp.float32)).astype(o.dtype)


def mixer(h, cos, sin, w_in, conv_w, conv_b, conv_ln_g, conv_ln_b, ret_gn_g, w_out):
    bsz, seq, _ = h.shape
    proj = h @ w_in
    splits = [D_CONV, 2 * D_CONV, 2 * D_CONV + D_RET, 2 * D_CONV + 2 * D_RET, 2 * D_CONV + 3 * D_RET]
    c_val, c_gate, q, k, v, g = jnp.split(proj, splits, axis=-1)

    u = c_val * jax.nn.sigmoid(c_gate)
    u = causal_dwconv(u, conv_w, conv_b)
    u = jax.nn.silu(layer_norm(u, conv_ln_g, conv_ln_b))

    q = apply_rotary(q.reshape(bsz, seq, N_RET_HEADS, RET_HEAD_DIM), cos, sin)
    k = apply_rotary(k.reshape(bsz, seq, N_RET_HEADS, RET_HEAD_DIM), cos, sin) * (RET_HEAD_DIM ** -0.5)
    v = v.reshape(bsz, seq, N_RET_HEADS, RET_HEAD_DIM)
    o = head_group_norm(chunk_retention(q, k, v), ret_gn_g)
    o = jax.nn.silu(g) * o

    return jnp.concatenate([u, o], axis=-1) @ w_out


def conv_gated_mlp(h, ffn_up, ffn_conv_w, ffn_conv_b, ffn_down):
    up = causal_dwconv(h @ ffn_up, ffn_conv_w, ffn_conv_b)
    val, gate = jnp.split(up, 2, axis=-1)
    return (jax.nn.gelu(gate, approximate=True) * val) @ ffn_down


def setup_inputs(seed: int = 0) -> dict:
    key = jax.random.key(seed)
    ks = jax.random.split(key, 24)
    f32 = jnp.float32
    L = DEPTH

    def nrm(k, shape, scale):
        return jax.random.normal(k, shape, f32) * scale

    def gain(k, shape):
        return 1.0 + 0.02 * jax.random.normal(k, shape, f32)

    x = jax.random.normal(ks[0], (BATCH, SEQ, D_MODEL), f32)
    p = jax.random.normal(ks[1], (DEPTH, BATCH, SEQ, PLE_DIM), f32)
    offset = jax.random.randint(ks[2], (BATCH, 1), 0, 4096, dtype=jnp.int32)
    positions = (offset + jnp.arange(SEQ, dtype=jnp.int32)[None, :]).astype(jnp.int32)
    return {
        "x": x,
        "p": p,
        "positions": positions,
        "norm_mix_pre": gain(ks[3], (L, D_MODEL)),
        "norm_mix_post": gain(ks[4], (L, D_MODEL)),
        "norm_ffn_pre": gain(ks[5], (L, D_MODEL)),
        "norm_ffn_post": gain(ks[6], (L, D_MODEL)),
        "w_in": nrm(ks[7], (L, D_MODEL, D_IN_PROJ), D_MODEL ** -0.5),
        "conv_w": nrm(ks[8], (L, CONV_WIDTH, D_CONV), CONV_WIDTH ** -0.5),
        "conv_b": nrm(ks[9], (L, D_CONV), 0.01),
        "conv_ln_g": gain(ks[10], (L, D_CONV)),
        "conv_ln_b": nrm(ks[11], (L, D_CONV), 0.01),
        "ret_gn_g": gain(ks[12], (L, D_RET)),
        "w_out": nrm(ks[13], (L, D_MIX, D_MODEL), D_MIX ** -0.5),
        "ffn_up": nrm(ks[14], (L, D_MODEL, 2 * D_FF), D_MODEL ** -0.5),
        "ffn_conv_w": nrm(ks[15], (L, FFN_CONV_WIDTH, 2 * D_FF), FFN_CONV_WIDTH ** -0.5),
        "ffn_conv_b": nrm(ks[16], (L, 2 * D_FF), 0.01),
        "ffn_down": nrm(ks[17], (L, D_FF, D_MODEL), D_FF ** -0.5),
        "ple_gate": nrm(ks[18], (L, D_MODEL, D_MODEL), D_MODEL ** -0.5),
        "ple_proj": nrm(ks[19], (L, PLE_DIM, D_MODEL), PLE_DIM ** -0.5),
    }


def reference(x, p, positions, norm_mix_pre, norm_mix_post, norm_ffn_pre, norm_ffn_post,
              w_in, conv_w, conv_b, conv_ln_g, conv_ln_b, ret_gn_g, w_out,
              ffn_up, ffn_conv_w, ffn_conv_b, ffn_down, ple_gate, ple_proj):
    cos, sin = rotary_tables(positions)
    for i in range(DEPTH):
        h = rms_norm(x, norm_mix_pre[i])
        m = mixer(h, cos, sin, w_in[i], conv_w[i], conv_b[i], conv_ln_g[i], conv_ln_b[i],
                  ret_gn_g[i], w_out[i])
        x = x + rms_norm(m, norm_mix_post[i])
        h = rms_norm(x, norm_ffn_pre[i])
        f = conv_gated_mlp(h, ffn_up[i], ffn_conv_w[i], ffn_conv_b[i], ffn_down[i])
        x = x + rms_norm(f, norm_ffn_post[i])
        x = x + jax.nn.sigmoid(x @ ple_gate[i]) * (p[i] @ ple_proj[i])
    return x
```

```python
import functools

import jax
import jax.numpy as jnp
from jax import lax
from jax.experimental import pallas as pl
from jax.experimental.pallas import tpu as pltpu

D_MODEL = 1024
PLE_DIM = 256
D_CONV = 512
CONV_WIDTH = 31
N_RET_HEADS = 4
RET_HEAD_DIM = 128
D_RET = N_RET_HEADS * RET_HEAD_DIM
D_MIX = D_CONV + D_RET
D_IN_PROJ = 2 * D_CONV + 4 * D_RET
RET_CHUNK = 128
ROPE_BASE = 10000.0
D_FF = 2816
FFN_CONV_WIDTH = 3
EPS = 1e-6

LANES = 128
SUBLANES = 8
VMEM_LIMIT_BYTES = 56 * 1024 * 1024

TM_MIX = 256
TM_FFN = 512
TS_ROT = 1024
CONV_HALO = 32
CONV_ROWS = 64
FFN_COLS = 256
FFN_ROWS = 64
FFN_HALO = SUBLANES

F32 = jnp.float32
BF16 = jnp.bfloat16


def _rms(x, g):
    ms = jnp.mean(x * x, axis=-1, keepdims=True)
    return x * lax.rsqrt(ms + EPS) * g


def _dot(a, b):
    return jnp.dot(a, b, preferred_element_type=F32)


def _rot_kernel(pos_ref, invf_ref, sign_ref, c2_ref, s2_ref):
    ang = pos_ref[...].astype(F32) * invf_ref[...]
    c2_ref[...] = jnp.cos(ang)
    s2_ref[...] = jnp.sin(ang) * sign_ref[...]


def _rotary_tables(positions):
    n = positions.size
    half = RET_HEAD_DIM // 2
    inv_freq = ROPE_BASE ** (-jnp.arange(half, dtype=F32) / half)
    invf2 = jnp.concatenate([inv_freq, inv_freq])[None, :]
    sign = jnp.concatenate([-jnp.ones((half,), F32), jnp.ones((half,), F32)])[None, :]
    row = pl.BlockSpec((1, RET_HEAD_DIM), lambda i: (0, 0))
    tab = pl.BlockSpec((TS_ROT, RET_HEAD_DIM), lambda i: (i, 0))
    return pl.pallas_call(
        _rot_kernel,
        grid=(n // TS_ROT,),
        in_specs=[pl.BlockSpec((TS_ROT, 1), lambda i: (i, 0)), row, row],
        out_specs=[tab, tab],
        out_shape=[jax.ShapeDtypeStruct((n, RET_HEAD_DIM), F32)] * 2,
        name="rotary_tables",
    )(positions.reshape(n, 1), invf2, sign)


def _mixer_kernel(x_ref, c2_ref, s2_ref, gpre_ref, gpost_ref, win_ref, cw_ref, cb_ref, lng_ref, lnb_ref,
                  gng_ref, wout_ref, din_ref, zeta_ref, xi_ref, cdec_ref, o_ref,
                  proj, ubuf, mixin, state):
    tm = x_ref.shape[0]

    @pl.when(pl.program_id(1) == 0)
    def _():
        ubuf[0:CONV_HALO, :] = jnp.zeros((CONV_HALO, D_CONV), F32)
        state[...] = jnp.zeros_like(state)

    h = _rms(x_ref[...], gpre_ref[...]).astype(BF16)
    proj[...] = _dot(h, win_ref[...])

    for r in range(0, tm, CONV_ROWS):
        rows = pl.ds(r, CONV_ROWS)
        ubuf[pl.ds(CONV_HALO + r, CONV_ROWS), :] = (
            proj[rows, 0:D_CONV] * jax.nn.sigmoid(proj[rows, D_CONV:2 * D_CONV]))
    first = CONV_HALO - (CONV_WIDTH - 1)
    for r in range(0, tm, CONV_ROWS):
        ys = []
        for c in range(0, D_CONV, LANES):
            cols = pl.ds(c, LANES)
            acc = jnp.broadcast_to(cb_ref[:, cols], (CONV_ROWS, LANES))
            for k in range(CONV_WIDTH):
                acc = acc + cw_ref[k:k + 1, cols] * ubuf[pl.ds(first + r + k, CONV_ROWS), cols]
            ys.append(acc)
        y = jnp.concatenate(ys, axis=-1)
        mu = jnp.mean(y, axis=-1, keepdims=True)
        yc = y - mu
        var = jnp.mean(yc * yc, axis=-1, keepdims=True)
        y = yc * lax.rsqrt(var + EPS) * lng_ref[...] + lnb_ref[...]
        mixin[pl.ds(r, CONV_ROWS), 0:D_CONV] = (y * jax.nn.sigmoid(y)).astype(BF16)
    ubuf[0:CONV_HALO, :] = ubuf[tm:tm + CONV_HALO, :]

    q0, k0, v0, g0 = 2 * D_CONV, 2 * D_CONV + D_RET, 2 * D_CONV + 2 * D_RET, 2 * D_CONV + 3 * D_RET
    for r in range(0, tm, RET_CHUNK):
        rows = pl.ds(r, RET_CHUNK)
        c2 = c2_ref[rows, :]
        s2 = s2_ref[rows, :]
        for hh in range(N_RET_HEADS):
            hc = hh * RET_HEAD_DIM
            qs = proj[rows, q0 + hc:q0 + hc + RET_HEAD_DIM]
            ks = proj[rows, k0 + hc:k0 + hc + RET_HEAD_DIM]
            vs = proj[rows, v0 + hc:v0 + hc + RET_HEAD_DIM]
            gs = proj[rows, g0 + hc:g0 + hc + RET_HEAD_DIM]
            qr = qs * c2 + pltpu.roll(qs, RET_HEAD_DIM // 2, 1) * s2
            kr = (ks * c2 + pltpu.roll(ks, RET_HEAD_DIM // 2, 1) * s2) * (RET_HEAD_DIM ** -0.5)
            qb = qr.astype(BF16)
            kb = kr.astype(BF16)
            scores = lax.dot_general(qb, kb, (((1,), (1,)), ((), ())), preferred_element_type=F32)
            scores = scores * din_ref[hh]
            inner = _dot(scores.astype(BF16), vs.astype(BF16))
            st = state[hh]
            cross = _dot((qr * xi_ref[hh]).astype(BF16), st.astype(BF16))
            kv = lax.dot_general(kb, (vs * zeta_ref[hh]).astype(BF16), (((0,), (0,)), ((), ())),
                                 preferred_element_type=F32)
            state[hh] = st * cdec_ref[hh] + kv
            o = inner + cross
            mu = jnp.mean(o, axis=-1, keepdims=True)
            oc = o - mu
            var = jnp.mean(oc * oc, axis=-1, keepdims=True)
            o = oc * lax.rsqrt(var + EPS) * gng_ref[:, hc:hc + RET_HEAD_DIM]
            mixin[rows, D_CONV + hc:D_CONV + hc + RET_HEAD_DIM] = (gs * jax.nn.sigmoid(gs) * o).astype(BF16)

    mix = _dot(mixin[...], wout_ref[...])
    o_ref[...] = x_ref[...] + _rms(mix, gpost_ref[...])


def _const_spec(shape):
    zeros = (0,) * len(shape)
    return pl.BlockSpec(shape, lambda b, t: zeros, pipeline_mode=pl.Buffered(1))


def _mixer_call(x, c2, s2, gpre, gpost, w_in, conv_w, conv_b, ln_g, ln_b, gn_g, w_out, din, zeta, xi, cdec):
    bsz, seq, _ = x.shape
    tm = TM_MIX
    tile = lambda d: pl.BlockSpec((None, tm, d), lambda b, t: (b, t, 0))
    consts = [gpre, gpost, w_in, conv_w, conv_b, ln_g, ln_b, gn_g, w_out, din, zeta, xi, cdec]
    return pl.pallas_call(
        _mixer_kernel,
        grid=(bsz, seq // tm),
        in_specs=[tile(D_MODEL), tile(RET_HEAD_DIM), tile(RET_HEAD_DIM)] + [_const_spec(a.shape) for a in consts],
        out_specs=tile(D_MODEL),
        out_shape=jax.ShapeDtypeStruct(x.shape, x.dtype),
        scratch_shapes=[
            pltpu.VMEM((tm, D_IN_PROJ), F32),
            pltpu.VMEM((CONV_HALO + tm, D_CONV), F32),
            pltpu.VMEM((tm, D_MIX), BF16),
            pltpu.VMEM((N_RET_HEADS, RET_HEAD_DIM, RET_HEAD_DIM), F32),
        ],
        compiler_params=pltpu.CompilerParams(
            dimension_semantics=("arbitrary", "arbitrary"), vmem_limit_bytes=VMEM_LIMIT_BYTES),
        name="mixer",
    )(x, c2, s2, *consts)


def _ffn_kernel(x_ref, p_ref, gpre_ref, gpost_ref, wup_ref, fcw_ref, fcb_ref, wdown_ref, wgate_ref, wple_ref,
                o_ref, hbuf, upbuf, halo, act):
    tm = x_ref.shape[0]

    @pl.when(pl.program_id(1) == 0)
    def _():
        halo[...] = jnp.zeros_like(halo)

    hbuf[...] = _rms(x_ref[...], gpre_ref[...]).astype(BF16)

    for j in range(0, D_FF, FFN_COLS):
        for half in range(2):
            col = half * D_FF + j
            cols = pl.ds(col, FFN_COLS)
            upbuf[half, 0:FFN_HALO, :] = halo[:, cols]
            upbuf[half, FFN_HALO:FFN_HALO + tm, :] = _dot(hbuf[...], wup_ref[:, cols])
            halo[:, cols] = upbuf[half, tm:tm + FFN_HALO, :]
        for r in range(0, tm, FFN_ROWS):
            conv = []
            for half in range(2):
                cols = pl.ds(half * D_FF + j, FFN_COLS)
                acc = jnp.broadcast_to(fcb_ref[:, cols], (FFN_ROWS, FFN_COLS))
                for k in range(FFN_CONV_WIDTH):
                    start = FFN_HALO - (FFN_CONV_WIDTH - 1) + r + k
                    acc = acc + fcw_ref[k:k + 1, cols] * upbuf[half, pl.ds(start, FFN_ROWS), :]
                conv.append(acc)
            val, gate = conv
            act[pl.ds(r, FFN_ROWS), pl.ds(j, FFN_COLS)] = (jax.nn.gelu(gate, approximate=True) * val).astype(BF16)

    f = _dot(act[...], wdown_ref[...])
    y = x_ref[...] + _rms(f, gpost_ref[...])
    gate = jax.nn.sigmoid(_dot(y.astype(BF16), wgate_ref[...]))
    o_ref[...] = y + gate * _dot(p_ref[...].astype(BF16), wple_ref[...])


def _ffn_call(x, p, gpre, gpost, w_up, fcw, fcb, w_down, w_gate, w_ple):
    bsz, seq, _ = x.shape
    tm = TM_FFN
    tile = lambda d: pl.BlockSpec((None, tm, d), lambda b, t: (b, t, 0))
    consts = [gpre, gpost, w_up, fcw, fcb, w_down, w_gate, w_ple]
    return pl.pallas_call(
        _ffn_kernel,
        grid=(bsz, seq // tm),
        in_specs=[tile(D_MODEL), tile(PLE_DIM)] + [_const_spec(a.shape) for a in consts],
        out_specs=tile(D_MODEL),
        out_shape=jax.ShapeDtypeStruct(x.shape, x.dtype),
        scratch_shapes=[
            pltpu.VMEM((tm, D_MODEL), BF16),
            pltpu.VMEM((2, FFN_HALO + tm, FFN_COLS), F32),
            pltpu.VMEM((FFN_HALO, 2 * D_FF), F32),
            pltpu.VMEM((tm, D_FF), BF16),
        ],
        compiler_params=pltpu.CompilerParams(
            dimension_semantics=("arbitrary", "arbitrary"), vmem_limit_bytes=VMEM_LIMIT_BYTES),
        name="ffn_ple",
    )(x, p, *consts)


def _retention_tables():
    n_h, c = N_RET_HEADS, RET_CHUNK
    log_g = jnp.log1p(-(2.0 ** (-5.0 - jnp.arange(n_h, dtype=F32))))
    idx = jnp.arange(c, dtype=F32)
    rel = idx[:, None] - idx[None, :]
    decay_in = jnp.where(rel >= 0, jnp.exp(log_g[:, None, None] * jnp.maximum(rel, 0.0)), 0.0)
    zeta = jnp.exp(log_g[:, None] * (c - 1 - idx))
    xi = jnp.exp(log_g[:, None] * (idx + 1))
    chunk_decay = jnp.exp(log_g * c)
    lanes = (n_h, c, RET_HEAD_DIM)
    return (decay_in, jnp.broadcast_to(zeta[:, :, None], lanes), jnp.broadcast_to(xi[:, :, None], lanes),
            jnp.broadcast_to(chunk_decay[:, None, None], (n_h, 1, RET_HEAD_DIM)))


def kernel(x, p, positions, norm_mix_pre, norm_mix_post, norm_ffn_pre, norm_ffn_post, w_in, conv_w, conv_b,
           conv_ln_g, conv_ln_b, ret_gn_g, w_out, ffn_up, ffn_conv_w, ffn_conv_b, ffn_down, ple_gate, ple_proj):
    bsz, seq, _ = x.shape
    depth = w_in.shape[0]
    c2, s2 = _rotary_tables(positions)
    c2 = c2.reshape(bsz, seq, RET_HEAD_DIM)
    s2 = s2.reshape(bsz, seq, RET_HEAD_DIM)
    din, zeta, xi, cdec = _retention_tables()
    row = lambda a: a[None, :]
    for i in range(depth):
        x = _mixer_call(x, c2, s2, row(norm_mix_pre[i]), row(norm_mix_post[i]), w_in[i].astype(BF16),
                        conv_w[i], row(conv_b[i]), row(conv_ln_g[i]), row(conv_ln_b[i]), row(ret_gn_g[i]),
                        w_out[i].astype(BF16), din, zeta, xi, cdec)
        x = _ffn_call(x, p[i], row(norm_ffn_pre[i]), row(norm_ffn_post[i]), ffn_up[i].astype(BF16),
                      ffn_conv_w[i], row(ffn_conv_b[i]), ffn_down[i].astype(BF16), ple_gate[i].astype(BF16),
                      ple_proj[i].astype(BF16))
    return x
```

```python
import jax
import jax.numpy as jnp
from jax import lax
from jax.experimental import pallas as pl
from jax.experimental.pallas import tpu as pltpu

D_MODEL = 1024
PLE_DIM = 256
D_CONV = 512
CONV_WIDTH = 31
N_RET_HEADS = 4
RET_HEAD_DIM = 128
D_RET = N_RET_HEADS * RET_HEAD_DIM
D_MIX = D_CONV + D_RET
D_IN_PROJ = 2 * D_CONV + 4 * D_RET
RET_CHUNK = 128
ROPE_BASE = 10000.0
D_FF = 2816
FFN_CONV_WIDTH = 3
EPS = 1e-6

LANES = 128
SUBLANES = 8
BF16_ROWS = 16
VMEM_LIMIT_BYTES = 56 * 1024 * 1024

TM_MIX = 256
TM_FFN = 512
TS_ROT = 1024
CONV_HALO = 32
CONV_ROWS = 64
FFN_COLS = 256
FFN_ROWS = 64
FFN_HALO = BF16_ROWS
GELU_C0 = 0.7978845608028654
GELU_C1 = 0.044715 * GELU_C0

F32 = jnp.float32
BF16 = jnp.bfloat16


def _rms(x, g):
    ms = jnp.mean(x * x, axis=-1, keepdims=True)
    return x * lax.rsqrt(ms + EPS) * g


def _dot(a, b):
    return jnp.dot(a, b, preferred_element_type=F32)


def _rot_kernel(pos_ref, invf_ref, sign_ref, c2_ref, s2_ref):
    ang = pos_ref[...].astype(F32) * invf_ref[...]
    c2_ref[...] = jnp.cos(ang)
    s2_ref[...] = jnp.sin(ang) * sign_ref[...]


def _rotary_tables(positions):
    n = positions.size
    half = RET_HEAD_DIM // 2
    inv_freq = ROPE_BASE ** (-jnp.arange(half, dtype=F32) / half)
    invf2 = jnp.concatenate([inv_freq, inv_freq])[None, :]
    sign = jnp.concatenate([-jnp.ones((half,), F32), jnp.ones((half,), F32)])[None, :]
    row = pl.BlockSpec((1, RET_HEAD_DIM), lambda i: (0, 0))
    tab = pl.BlockSpec((TS_ROT, RET_HEAD_DIM), lambda i: (i, 0))
    return pl.pallas_call(
        _rot_kernel,
        grid=(n // TS_ROT,),
        in_specs=[pl.BlockSpec((TS_ROT, 1), lambda i: (i, 0)), row, row],
        out_specs=[tab, tab],
        out_shape=[jax.ShapeDtypeStruct((n, RET_HEAD_DIM), F32)] * 2,
        name="rotary_tables",
    )(positions.reshape(n, 1), invf2, sign)


def _mixer_kernel(x_ref, c2_ref, s2_ref, gpre_ref, gpost_ref, win_ref, cw_ref, cb_ref, lng_ref, lnb_ref,
                  gng_ref, wout_ref, din_ref, zeta_ref, xi_ref, cdec_ref, o_ref,
                  proj, ush, mixin, state):
    tm = x_ref.shape[0]
    n_u = CONV_HALO + tm

    @pl.when(pl.program_id(1) == 0)
    def _():
        ush[0, 0:CONV_HALO, :] = jnp.zeros((CONV_HALO, D_CONV), F32)
        state[...] = jnp.zeros_like(state)

    h = _rms(x_ref[...], gpre_ref[...]).astype(BF16)
    proj[...] = _dot(h, win_ref[...])

    for r in range(0, tm, CONV_ROWS):
        rows = pl.ds(r, CONV_ROWS)
        ush[0, pl.ds(CONV_HALO + r, CONV_ROWS), :] = (
            proj[rows, 0:D_CONV] * jax.nn.sigmoid(proj[rows, D_CONV:2 * D_CONV]))
    for c in range(0, D_CONV, LANES):
        cols = pl.ds(c, LANES)
        u = ush[0, :, cols]
        for s in range(1, SUBLANES):
            ush[s, :, cols] = pltpu.roll(u, n_u - s, 0)
    first = CONV_HALO - (CONV_WIDTH - 1)
    for r in range(0, tm, CONV_ROWS):
        ys = []
        for c in range(0, D_CONV, LANES):
            cols = pl.ds(c, LANES)
            acc = jnp.broadcast_to(cb_ref[:, cols], (CONV_ROWS, LANES))
            for k in range(CONV_WIDTH):
                off = first + k
                acc = acc + cw_ref[k:k + 1, cols] * ush[off % SUBLANES,
                                                         pl.ds(r + off - off % SUBLANES, CONV_ROWS), cols]
            ys.append(acc)
        y = jnp.concatenate(ys, axis=-1)
        mu = jnp.mean(y, axis=-1, keepdims=True)
        yc = y - mu
        var = jnp.mean(yc * yc, axis=-1, keepdims=True)
        y = yc * lax.rsqrt(var + EPS) * lng_ref[...] + lnb_ref[...]
        mixin[pl.ds(r, CONV_ROWS), 0:D_CONV] = (y * jax.nn.sigmoid(y)).astype(BF16)
    ush[0, 0:CONV_HALO, :] = ush[0, tm:tm + CONV_HALO, :]

    q0, k0, v0, g0 = 2 * D_CONV, 2 * D_CONV + D_RET, 2 * D_CONV + 2 * D_RET, 2 * D_CONV + 3 * D_RET
    for r in range(0, tm, RET_CHUNK):
        rows = pl.ds(r, RET_CHUNK)
        c2 = c2_ref[rows, :]
        s2 = s2_ref[rows, :]
        for hh in range(N_RET_HEADS):
            hc = hh * RET_HEAD_DIM
            qs = proj[rows, q0 + hc:q0 + hc + RET_HEAD_DIM]
            ks = proj[rows, k0 + hc:k0 + hc + RET_HEAD_DIM]
            vs = proj[rows, v0 + hc:v0 + hc + RET_HEAD_DIM]
            gs = proj[rows, g0 + hc:g0 + hc + RET_HEAD_DIM]
            qr = qs * c2 + pltpu.roll(qs, RET_HEAD_DIM // 2, 1) * s2
            kr = (ks * c2 + pltpu.roll(ks, RET_HEAD_DIM // 2, 1) * s2) * (RET_HEAD_DIM ** -0.5)
            qb = qr.astype(BF16)
            kb = kr.astype(BF16)
            scores = lax.dot_general(qb, kb, (((1,), (1,)), ((), ())), preferred_element_type=F32)
            scores = scores * din_ref[hh]
            inner = _dot(scores.astype(BF16), vs.astype(BF16))
            st = state[hh]
            cross = _dot((qr * xi_ref[hh]).astype(BF16), st.astype(BF16))
            kv = lax.dot_general(kb, (vs * zeta_ref[hh]).astype(BF16), (((0,), (0,)), ((), ())),
                                 preferred_element_type=F32)
            state[hh] = st * cdec_ref[hh] + kv
            o = inner + cross
            mu = jnp.mean(o, axis=-1, keepdims=True)
            oc = o - mu
            var = jnp.mean(oc * oc, axis=-1, keepdims=True)
            o = oc * lax.rsqrt(var + EPS) * gng_ref[:, hc:hc + RET_HEAD_DIM]
            mixin[rows, D_CONV + hc:D_CONV + hc + RET_HEAD_DIM] = (gs * jax.nn.sigmoid(gs) * o).astype(BF16)

    mix = _dot(mixin[...], wout_ref[...])
    o_ref[...] = x_ref[...] + _rms(mix, gpost_ref[...])


def _layer_spec(a, layer):
    zeros = (0,) * (a.ndim - 1)
    return pl.BlockSpec((None,) + a.shape[1:], lambda b, t: (layer,) + zeros, pipeline_mode=pl.Buffered(1))


def _const_spec(a):
    zeros = (0,) * a.ndim
    return pl.BlockSpec(a.shape, lambda b, t: zeros, pipeline_mode=pl.Buffered(1))


def _mixer_call(layer, x, c2, s2, params, tables):
    bsz, seq, _ = x.shape
    tm = TM_MIX
    tile = lambda d: pl.BlockSpec((None, tm, d), lambda b, t: (b, t, 0))
    return pl.pallas_call(
        _mixer_kernel,
        grid=(bsz, seq // tm),
        in_specs=([tile(D_MODEL), tile(RET_HEAD_DIM), tile(RET_HEAD_DIM)]
                  + [_layer_spec(a, layer) for a in params] + [_const_spec(a) for a in tables]),
        out_specs=tile(D_MODEL),
        out_shape=jax.ShapeDtypeStruct(x.shape, x.dtype),
        scratch_shapes=[
            pltpu.VMEM((tm, D_IN_PROJ), F32),
            pltpu.VMEM((SUBLANES, CONV_HALO + tm, D_CONV), F32),
            pltpu.VMEM((tm, D_MIX), BF16),
            pltpu.VMEM((N_RET_HEADS, RET_HEAD_DIM, RET_HEAD_DIM), F32),
        ],
        compiler_params=pltpu.CompilerParams(
            dimension_semantics=("arbitrary", "arbitrary"), vmem_limit_bytes=VMEM_LIMIT_BYTES),
        name="mixer",
    )(x, c2, s2, *params, *tables)


def _ffn_kernel(x_ref, p_ref, gpre_ref, gpost_ref, wup_ref, fcw_ref, fcb_ref, wdown_ref, wgate_ref, wple_ref,
                o_ref, hbuf, upv, upg, act):
    tm = x_ref.shape[0]

    @pl.when(pl.program_id(1) == 0)
    def _():
        hbuf[0:FFN_HALO, :] = jnp.zeros((FFN_HALO, D_MODEL), BF16)

    @pl.when(pl.program_id(1) != 0)
    def _():
        hbuf[0:FFN_HALO, :] = hbuf[tm:tm + FFN_HALO, :]

    hbuf[FFN_HALO:FFN_HALO + tm, :] = _rms(x_ref[...], gpre_ref[...]).astype(BF16)

    lead = FFN_HALO - SUBLANES
    for jn, j in enumerate(range(0, D_FF, FFN_COLS)):
        slot = jn % 2
        upv[slot, :, :] = _dot(hbuf[...], wup_ref[:, pl.ds(j, FFN_COLS)])
        upg[slot, :, :] = _dot(hbuf[...], wup_ref[:, pl.ds(D_FF + j, FFN_COLS)])
        for r in range(0, tm, FFN_ROWS):
            conv = []
            for ub, col in ((upv, j), (upg, D_FF + j)):
                cols = pl.ds(col, FFN_COLS)
                a = ub[slot, pl.ds(lead + r, FFN_ROWS + SUBLANES), :]
                s1 = pltpu.roll(a, 1, 0)[SUBLANES:]
                s2 = pltpu.roll(a, 2, 0)[SUBLANES:]
                conv.append(fcb_ref[:, cols] + fcw_ref[2:3, cols] * a[SUBLANES:]
                            + fcw_ref[1:2, cols] * s1 + fcw_ref[0:1, cols] * s2)
            val, gate = conv
            th = jnp.tanh(gate * (GELU_C0 + GELU_C1 * (gate * gate)))
            hg = 0.5 * gate
            act[pl.ds(r, FFN_ROWS), pl.ds(j, FFN_COLS)] = ((hg + hg * th) * val).astype(BF16)

    f = _dot(act[...], wdown_ref[...])
    y = x_ref[...] + _rms(f, gpost_ref[...])
    gate = jax.nn.sigmoid(_dot(y.astype(BF16), wgate_ref[...]))
    o_ref[...] = y + gate * _dot(p_ref[...].astype(BF16), wple_ref[...])


def _ffn_call(layer, x, p, params):
    bsz, seq, _ = x.shape
    tm = TM_FFN
    tile = lambda d: pl.BlockSpec((None, tm, d), lambda b, t: (b, t, 0))
    p_tile = pl.BlockSpec((None, None, tm, PLE_DIM), lambda b, t: (layer, b, t, 0))
    up_block = pltpu.VMEM((2, FFN_HALO + tm, FFN_COLS), F32)
    return pl.pallas_call(
        _ffn_kernel,
        grid=(bsz, seq // tm),
        in_specs=[tile(D_MODEL), p_tile] + [_layer_spec(a, layer) for a in params],
        out_specs=tile(D_MODEL),
        out_shape=jax.ShapeDtypeStruct(x.shape, x.dtype),
        scratch_shapes=[
            pltpu.VMEM((FFN_HALO + tm, D_MODEL), BF16),
            up_block, up_block,
            pltpu.VMEM((tm, D_FF), BF16),
        ],
        compiler_params=pltpu.CompilerParams(
            dimension_semantics=("arbitrary", "arbitrary"), vmem_limit_bytes=VMEM_LIMIT_BYTES),
        name="ffn_ple",
    )(x, p, *params)


def _retention_tables():
    n_h, c = N_RET_HEADS, RET_CHUNK
    log_g = jnp.log1p(-(2.0 ** (-5.0 - jnp.arange(n_h, dtype=F32))))
    idx = jnp.arange(c, dtype=F32)
    rel = idx[:, None] - idx[None, :]
    decay_in = jnp.where(rel >= 0, jnp.exp(log_g[:, None, None] * jnp.maximum(rel, 0.0)), 0.0)
    zeta = jnp.exp(log_g[:, None] * (c - 1 - idx))
    xi = jnp.exp(log_g[:, None] * (idx + 1))
    chunk_decay = jnp.exp(log_g * c)
    lanes = (n_h, c, RET_HEAD_DIM)
    return (decay_in, jnp.broadcast_to(zeta[:, :, None], lanes), jnp.broadcast_to(xi[:, :, None], lanes),
            jnp.broadcast_to(chunk_decay[:, None, None], (n_h, 1, RET_HEAD_DIM)))


def kernel(x, p, positions, norm_mix_pre, norm_mix_post, norm_ffn_pre, norm_ffn_post, w_in, conv_w, conv_b,
           conv_ln_g, conv_ln_b, ret_gn_g, w_out, ffn_up, ffn_conv_w, ffn_conv_b, ffn_down, ple_gate, ple_proj):
    bsz, seq, _ = x.shape
    depth = w_in.shape[0]
    c2, s2 = _rotary_tables(positions)
    c2 = c2.reshape(bsz, seq, RET_HEAD_DIM)
    s2 = s2.reshape(bsz, seq, RET_HEAD_DIM)
    tables = _retention_tables()
    row = lambda a: a[:, None, :]
    mixer_params = (row(norm_mix_pre), row(norm_mix_post), w_in.astype(BF16), conv_w, row(conv_b),
                    row(conv_ln_g), row(conv_ln_b), row(ret_gn_g), w_out.astype(BF16))
    ffn_params = (row(norm_ffn_pre), row(norm_ffn_post), ffn_up.astype(BF16), ffn_conv_w, row(ffn_conv_b),
                  ffn_down.astype(BF16), ple_gate.astype(BF16), ple_proj.astype(BF16))
    for layer in range(depth):
        x = _mixer_call(layer, x, c2, s2, mixer_params, tables)
        x = _ffn_call(layer, x, p, ffn_params)
    return x
```

```python
import jax
import jax.numpy as jnp
from jax import lax
from jax.experimental import pallas as pl
from jax.experimental.pallas import tpu as pltpu

D_MODEL = 1024
PLE_DIM = 256
D_CONV = 512
CONV_WIDTH = 31
N_RET_HEADS = 4
RET_HEAD_DIM = 128
D_RET = N_RET_HEADS * RET_HEAD_DIM
D_MIX = D_CONV + D_RET
D_IN_PROJ = 2 * D_CONV + 4 * D_RET
RET_CHUNK = 128
ROPE_BASE = 10000.0
D_FF = 2816
FFN_CONV_WIDTH = 3
EPS = 1e-6

LANES = 128
SUBLANES = 8
BF16_ROWS = 16
VMEM_LIMIT_BYTES = 56 * 1024 * 1024

TM_MIX = 512
TM_FFN = 1024
TS_ROT = 1024
CONV_HALO = 32
CONV_ROWS = 64
FFN_COLS = 256
FFN_ROWS = 64
FFN_HALO = BF16_ROWS
GELU_C0 = 0.7978845608028654
GELU_C1 = 0.044715 * GELU_C0

F32 = jnp.float32
BF16 = jnp.bfloat16


def _rms(x, g):
    ms = jnp.mean(x * x, axis=-1, keepdims=True)
    return x * lax.rsqrt(ms + EPS) * g


def _dot(a, b):
    return jnp.dot(a, b, preferred_element_type=F32)


def _rot_kernel(pos_ref, invf_ref, sign_ref, c2_ref, s2_ref):
    ang = pos_ref[...].astype(F32) * invf_ref[...]
    c2_ref[...] = jnp.cos(ang)
    s2_ref[...] = jnp.sin(ang) * sign_ref[...]


def _rotary_tables(positions):
    n = positions.size
    half = RET_HEAD_DIM // 2
    inv_freq = ROPE_BASE ** (-jnp.arange(half, dtype=F32) / half)
    invf2 = jnp.concatenate([inv_freq, inv_freq])[None, :]
    sign = jnp.concatenate([-jnp.ones((half,), F32), jnp.ones((half,), F32)])[None, :]
    row = pl.BlockSpec((1, RET_HEAD_DIM), lambda i: (0, 0))
    tab = pl.BlockSpec((TS_ROT, RET_HEAD_DIM), lambda i: (i, 0))
    return pl.pallas_call(
        _rot_kernel,
        grid=(n // TS_ROT,),
        in_specs=[pl.BlockSpec((TS_ROT, 1), lambda i: (i, 0)), row, row],
        out_specs=[tab, tab],
        out_shape=[jax.ShapeDtypeStruct((n, RET_HEAD_DIM), F32)] * 2,
        name="rotary_tables",
    )(positions.reshape(n, 1), invf2, sign)


def _mixer_kernel(x_ref, c2_ref, s2_ref, gpre_ref, gpost_ref, win_ref, cw_ref, cb_ref, lng_ref, lnb_ref,
                  gng_ref, wout_ref, din_ref, zeta_ref, xi_ref, cdec_ref, o_ref,
                  proj, ush, mixin, state, win, wout):
    tm = x_ref.shape[0]
    n_u = CONV_HALO + tm

    @pl.when(pl.program_id(1) == 0)
    def _():
        ush[0, 0:CONV_HALO, :] = jnp.zeros((CONV_HALO, D_CONV), F32)
        state[...] = jnp.zeros_like(state)

    @pl.when((pl.program_id(0) == 0) & (pl.program_id(1) == 0))
    def _():
        win[...] = win_ref[...]
        wout[...] = wout_ref[...]

    h = _rms(x_ref[...], gpre_ref[...]).astype(BF16)
    proj[...] = _dot(h, win[...])

    for r in range(0, tm, CONV_ROWS):
        rows = pl.ds(r, CONV_ROWS)
        ush[0, pl.ds(CONV_HALO + r, CONV_ROWS), :] = (
            proj[rows, 0:D_CONV] * jax.nn.sigmoid(proj[rows, D_CONV:2 * D_CONV]))
    for c in range(0, D_CONV, LANES):
        cols = pl.ds(c, LANES)
        u = ush[0, :, cols]
        for s in range(1, SUBLANES):
            ush[s, :, cols] = pltpu.roll(u, n_u - s, 0)
    first = CONV_HALO - (CONV_WIDTH - 1)
    for r in range(0, tm, CONV_ROWS):
        ys = []
        for c in range(0, D_CONV, LANES):
            cols = pl.ds(c, LANES)
            acc = jnp.broadcast_to(cb_ref[:, cols], (CONV_ROWS, LANES))
            for k in range(CONV_WIDTH):
                off = first + k
                acc = acc + cw_ref[k:k + 1, cols] * ush[off % SUBLANES,
                                                         pl.ds(r + off - off % SUBLANES, CONV_ROWS), cols]
            ys.append(acc)
        y = jnp.concatenate(ys, axis=-1)
        mu = jnp.mean(y, axis=-1, keepdims=True)
        yc = y - mu
        var = jnp.mean(yc * yc, axis=-1, keepdims=True)
        y = yc * lax.rsqrt(var + EPS) * lng_ref[...] + lnb_ref[...]
        mixin[pl.ds(r, CONV_ROWS), 0:D_CONV] = (y * jax.nn.sigmoid(y)).astype(BF16)
    ush[0, 0:CONV_HALO, :] = ush[0, tm:tm + CONV_HALO, :]

    q0, k0, v0, g0 = 2 * D_CONV, 2 * D_CONV + D_RET, 2 * D_CONV + 2 * D_RET, 2 * D_CONV + 3 * D_RET
    for r in range(0, tm, RET_CHUNK):
        rows = pl.ds(r, RET_CHUNK)
        c2 = c2_ref[rows, :]
        s2 = s2_ref[rows, :]
        for hh in range(N_RET_HEADS):
            hc = hh * RET_HEAD_DIM
            qs = proj[rows, q0 + hc:q0 + hc + RET_HEAD_DIM]
            ks = proj[rows, k0 + hc:k0 + hc + RET_HEAD_DIM]
            vs = proj[rows, v0 + hc:v0 + hc + RET_HEAD_DIM]
            gs = proj[rows, g0 + hc:g0 + hc + RET_HEAD_DIM]
            qr = qs * c2 + pltpu.roll(qs, RET_HEAD_DIM // 2, 1) * s2
            kr = ks * c2 + pltpu.roll(ks, RET_HEAD_DIM // 2, 1) * s2
            qb = qr.astype(BF16)
            kb = kr.astype(BF16)
            scores = lax.dot_general(qb, kb, (((1,), (1,)), ((), ())), preferred_element_type=F32)
            scores = scores * din_ref[hh]
            inner = _dot(scores.astype(BF16), vs.astype(BF16))
            st = state[hh]
            cross = _dot((qr * xi_ref[hh]).astype(BF16), st.astype(BF16))
            kv = lax.dot_general(kb, (vs * zeta_ref[hh]).astype(BF16), (((0,), (0,)), ((), ())),
                                 preferred_element_type=F32)
            state[hh] = st * cdec_ref[hh] + kv
            o = inner + cross
            mu = jnp.mean(o, axis=-1, keepdims=True)
            oc = o - mu
            var = jnp.mean(oc * oc, axis=-1, keepdims=True)
            o = oc * lax.rsqrt(var + EPS) * gng_ref[:, hc:hc + RET_HEAD_DIM]
            mixin[rows, D_CONV + hc:D_CONV + hc + RET_HEAD_DIM] = (gs * jax.nn.sigmoid(gs) * o).astype(BF16)

    mix = _dot(mixin[...], wout[...])
    o_ref[...] = x_ref[...] + _rms(mix, gpost_ref[...])


def _layer_spec(a, layer):
    zeros = (0,) * (a.ndim - 1)
    return pl.BlockSpec((None,) + a.shape[1:], lambda b, t: (layer,) + zeros, pipeline_mode=pl.Buffered(1))


def _const_spec(a):
    zeros = (0,) * a.ndim
    return pl.BlockSpec(a.shape, lambda b, t: zeros, pipeline_mode=pl.Buffered(1))


def _mixer_call(layer, x, c2, s2, params, tables):
    bsz, seq, _ = x.shape
    tm = TM_MIX
    tile = lambda d: pl.BlockSpec((None, tm, d), lambda b, t: (b, t, 0))
    return pl.pallas_call(
        _mixer_kernel,
        grid=(bsz, seq // tm),
        in_specs=([tile(D_MODEL), tile(RET_HEAD_DIM), tile(RET_HEAD_DIM)]
                  + [_layer_spec(a, layer) for a in params] + [_const_spec(a) for a in tables]),
        out_specs=tile(D_MODEL),
        out_shape=jax.ShapeDtypeStruct(x.shape, x.dtype),
        scratch_shapes=[
            pltpu.VMEM((tm, D_IN_PROJ), F32),
            pltpu.VMEM((SUBLANES, CONV_HALO + tm, D_CONV), F32),
            pltpu.VMEM((tm, D_MIX), BF16),
            pltpu.VMEM((N_RET_HEADS, RET_HEAD_DIM, RET_HEAD_DIM), F32),
            pltpu.VMEM((D_MODEL, D_IN_PROJ), BF16),
            pltpu.VMEM((D_MIX, D_MODEL), BF16),
        ],
        compiler_params=pltpu.CompilerParams(
            dimension_semantics=("arbitrary", "arbitrary"), vmem_limit_bytes=VMEM_LIMIT_BYTES),
        name="mixer",
    )(x, c2, s2, *params, *tables)


def _ffn_kernel(x_ref, p_ref, gpre_ref, gpost_ref, wup_ref, fcw_ref, fcb_ref, wdown_ref, wgate_ref, wple_ref,
                o_ref, hbuf, upv, upg, act):
    tm = x_ref.shape[0]

    @pl.when(pl.program_id(1) == 0)
    def _():
        hbuf[0:FFN_HALO, :] = jnp.zeros((FFN_HALO, D_MODEL), BF16)

    @pl.when(pl.program_id(1) != 0)
    def _():
        hbuf[0:FFN_HALO, :] = hbuf[tm:tm + FFN_HALO, :]

    hbuf[FFN_HALO:FFN_HALO + tm, :] = _rms(x_ref[...], gpre_ref[...]).astype(BF16)

    lead = FFN_HALO - SUBLANES
    for jn, j in enumerate(range(0, D_FF, FFN_COLS)):
        slot = jn % 2
        upv[slot, :, :] = _dot(hbuf[...], wup_ref[:, pl.ds(j, FFN_COLS)])
        upg[slot, :, :] = _dot(hbuf[...], wup_ref[:, pl.ds(D_FF + j, FFN_COLS)])
        for r in range(0, tm, FFN_ROWS):
            conv = []
            for ub, col in ((upv, j), (upg, D_FF + j)):
                cols = pl.ds(col, FFN_COLS)
                a = ub[slot, pl.ds(lead + r, FFN_ROWS + SUBLANES), :]
                s1 = pltpu.roll(a, 1, 0)[SUBLANES:]
                s2 = pltpu.roll(a, 2, 0)[SUBLANES:]
                conv.append(fcb_ref[:, cols] + fcw_ref[2:3, cols] * a[SUBLANES:]
                            + fcw_ref[1:2, cols] * s1 + fcw_ref[0:1, cols] * s2)
            val, gate = conv
            th = jnp.tanh(gate * (GELU_C0 + GELU_C1 * (gate * gate)))
            hg = 0.5 * gate
            act[pl.ds(r, FFN_ROWS), pl.ds(j, FFN_COLS)] = ((hg + hg * th) * val).astype(BF16)

    f = _dot(act[...], wdown_ref[...])
    y = x_ref[...] + _rms(f, gpost_ref[...])
    gate = jax.nn.sigmoid(_dot(y.astype(BF16), wgate_ref[...]))
    o_ref[...] = y + gate * _dot(p_ref[...].astype(BF16), wple_ref[...])


def _ffn_call(layer, x, p, params):
    bsz, seq, _ = x.shape
    tm = TM_FFN
    tile = lambda d: pl.BlockSpec((None, tm, d), lambda b, t: (b, t, 0))
    p_tile = pl.BlockSpec((None, None, tm, PLE_DIM), lambda b, t: (layer, b, t, 0))
    up_block = pltpu.VMEM((2, FFN_HALO + tm, FFN_COLS), F32)
    return pl.pallas_call(
        _ffn_kernel,
        grid=(bsz, seq // tm),
        in_specs=[tile(D_MODEL), p_tile] + [_layer_spec(a, layer) for a in params],
        out_specs=tile(D_MODEL),
        out_shape=jax.ShapeDtypeStruct(x.shape, x.dtype),
        scratch_shapes=[
            pltpu.VMEM((FFN_HALO + tm, D_MODEL), BF16),
            up_block, up_block,
            pltpu.VMEM((tm, D_FF), BF16),
        ],
        compiler_params=pltpu.CompilerParams(
            dimension_semantics=("arbitrary", "arbitrary"), vmem_limit_bytes=VMEM_LIMIT_BYTES),
        name="ffn_ple",
    )(x, p, *params)


def _retention_tables():
    n_h, c = N_RET_HEADS, RET_CHUNK
    log_g = jnp.log1p(-(2.0 ** (-5.0 - jnp.arange(n_h, dtype=F32))))
    idx = jnp.arange(c, dtype=F32)
    rel = idx[:, None] - idx[None, :]
    decay_in = jnp.where(rel >= 0, jnp.exp(log_g[:, None, None] * jnp.maximum(rel, 0.0)), 0.0)
    zeta = jnp.exp(log_g[:, None] * (c - 1 - idx))
    xi = jnp.exp(log_g[:, None] * (idx + 1))
    chunk_decay = jnp.exp(log_g * c)
    lanes = (n_h, c, RET_HEAD_DIM)
    key_scale = RET_HEAD_DIM ** -0.5
    decay_in = decay_in * key_scale
    zeta = zeta * key_scale
    return (decay_in, jnp.broadcast_to(zeta[:, :, None], lanes), jnp.broadcast_to(xi[:, :, None], lanes),
            jnp.broadcast_to(chunk_decay[:, None, None], (n_h, 1, RET_HEAD_DIM)))


def kernel(x, p, positions, norm_mix_pre, norm_mix_post, norm_ffn_pre, norm_ffn_post, w_in, conv_w, conv_b,
           conv_ln_g, conv_ln_b, ret_gn_g, w_out, ffn_up, ffn_conv_w, ffn_conv_b, ffn_down, ple_gate, ple_proj):
    bsz, seq, _ = x.shape
    depth = w_in.shape[0]
    c2, s2 = _rotary_tables(positions)
    c2 = c2.reshape(bsz, seq, RET_HEAD_DIM)
    s2 = s2.reshape(bsz, seq, RET_HEAD_DIM)
    tables = _retention_tables()
    row = lambda a: a[:, None, :]
    mixer_params = (row(norm_mix_pre), row(norm_mix_post), w_in.astype(BF16), conv_w, row(conv_b),
                    row(conv_ln_g), row(conv_ln_b), row(ret_gn_g), w_out.astype(BF16))
    ffn_params = (row(norm_ffn_pre), row(norm_ffn_post), ffn_up.astype(BF16), ffn_conv_w, row(ffn_conv_b),
                  ffn_down.astype(BF16), ple_gate.astype(BF16), ple_proj.astype(BF16))
    for layer in range(depth):
        x = _mixer_call(layer, x, c2, s2, mixer_params, tables)
        x = _ffn_call(layer, x, p, ffn_params)
    return x
```

```python
import jax
import jax.numpy as jnp
from jax import lax
from jax.experimental import pallas as pl
from jax.experimental.pallas import tpu as pltpu

D_MODEL = 1024
PLE_DIM = 256
D_CONV = 512
CONV_WIDTH = 31
N_RET_HEADS = 4
RET_HEAD_DIM = 128
D_RET = N_RET_HEADS * RET_HEAD_DIM
D_MIX = D_CONV + D_RET
D_IN_PROJ = 2 * D_CONV + 4 * D_RET
RET_CHUNK = 128
ROPE_BASE = 10000.0
D_FF = 2816
FFN_CONV_WIDTH = 3
EPS = 1e-6

LANES = 128
SUBLANES = 8
BF16_ROWS = 16
VMEM_LIMIT_BYTES = 56 * 1024 * 1024

TM_MIX = 512
TM_FFN = 512
TS_ROT = 1024
CONV_HALO = 32
CONV_ROWS = 64
FFN_COLS = 256
FFN_ROWS = 64
FFN_HALO = BF16_ROWS
GELU_C0 = 0.7978845608028654
GELU_C1 = 0.044715 * GELU_C0

F32 = jnp.float32
BF16 = jnp.bfloat16


def _rms(x, g):
    ms = jnp.mean(x * x, axis=-1, keepdims=True)
    return x * lax.rsqrt(ms + EPS) * g


def _dot(a, b):
    return jnp.dot(a, b, preferred_element_type=F32)


def _rot_kernel(pos_ref, invf_ref, sign_ref, c2_ref, s2_ref):
    ang = pos_ref[...].astype(F32) * invf_ref[...]
    c2_ref[...] = jnp.cos(ang)
    s2_ref[...] = jnp.sin(ang) * sign_ref[...]


def _rotary_tables(positions):
    n = positions.size
    half = RET_HEAD_DIM // 2
    inv_freq = ROPE_BASE ** (-jnp.arange(half, dtype=F32) / half)
    invf2 = jnp.concatenate([inv_freq, inv_freq])[None, :]
    sign = jnp.concatenate([-jnp.ones((half,), F32), jnp.ones((half,), F32)])[None, :]
    row = pl.BlockSpec((1, RET_HEAD_DIM), lambda i: (0, 0))
    tab = pl.BlockSpec((TS_ROT, RET_HEAD_DIM), lambda i: (i, 0))
    return pl.pallas_call(
        _rot_kernel,
        grid=(n // TS_ROT,),
        in_specs=[pl.BlockSpec((TS_ROT, 1), lambda i: (i, 0)), row, row],
        out_specs=[tab, tab],
        out_shape=[jax.ShapeDtypeStruct((n, RET_HEAD_DIM), F32)] * 2,
        name="rotary_tables",
    )(positions.reshape(n, 1), invf2, sign)


def _mixer_kernel(x_ref, c2_ref, s2_ref, gpre_ref, gpost_ref, win_ref, cw_ref, cb_ref, lng_ref, lnb_ref,
                  gng_ref, wout_ref, din_ref, zeta_ref, xi_ref, cdec_ref, o_ref,
                  proj, ush, mixin, state, win, wout, ybuf):
    tm = x_ref.shape[0]
    n_u = CONV_HALO + tm

    @pl.when(pl.program_id(1) == 0)
    def _():
        ush[0, 0:CONV_HALO, :] = jnp.zeros((CONV_HALO, D_CONV), F32)
        state[...] = jnp.zeros_like(state)

    @pl.when((pl.program_id(0) == 0) & (pl.program_id(1) == 0))
    def _():
        win[...] = win_ref[...]
        wout[...] = wout_ref[...]

    h = _rms(x_ref[...], gpre_ref[...]).astype(BF16)
    proj[...] = _dot(h, win[...])

    first = CONV_HALO - (CONV_WIDTH - 1)
    for c in range(0, D_CONV, LANES):
        cols = pl.ds(c, LANES)
        for r in range(0, tm, CONV_ROWS):
            rows = pl.ds(r, CONV_ROWS)
            ush[0, pl.ds(CONV_HALO + r, CONV_ROWS), cols] = (
                proj[rows, 2 * c:2 * c + LANES] * jax.nn.sigmoid(proj[rows, 2 * c + LANES:2 * c + 2 * LANES]))
        u = ush[0, :, cols]
        for s in range(1, SUBLANES):
            ush[s, :, cols] = pltpu.roll(u, n_u - s, 0)
        for r in range(0, tm, CONV_ROWS):
            acc = jnp.broadcast_to(cb_ref[:, cols], (CONV_ROWS, LANES))
            for k in range(CONV_WIDTH):
                off = first + k
                acc = acc + cw_ref[k:k + 1, cols] * ush[off % SUBLANES,
                                                         pl.ds(r + off - off % SUBLANES, CONV_ROWS), cols]
            ybuf[pl.ds(r, CONV_ROWS), cols] = acc
    for r in range(0, tm, CONV_ROWS):
        y = ybuf[pl.ds(r, CONV_ROWS), :]
        mu = jnp.mean(y, axis=-1, keepdims=True)
        yc = y - mu
        var = jnp.mean(yc * yc, axis=-1, keepdims=True)
        y = yc * lax.rsqrt(var + EPS) * lng_ref[...] + lnb_ref[...]
        mixin[pl.ds(r, CONV_ROWS), 0:D_CONV] = (y * jax.nn.sigmoid(y)).astype(BF16)
    ush[0, 0:CONV_HALO, :] = ush[0, tm:tm + CONV_HALO, :]
    mix_conv = _dot(mixin[:, 0:D_CONV], wout[0:D_CONV, :])

    q0, k0, v0, g0 = 2 * D_CONV, 2 * D_CONV + D_RET, 2 * D_CONV + 2 * D_RET, 2 * D_CONV + 3 * D_RET
    for r in range(0, tm, RET_CHUNK):
        rows = pl.ds(r, RET_CHUNK)
        c2 = c2_ref[rows, :]
        s2 = s2_ref[rows, :]
        for hh in range(N_RET_HEADS):
            hc = hh * RET_HEAD_DIM
            qs = proj[rows, q0 + hc:q0 + hc + RET_HEAD_DIM]
            ks = proj[rows, k0 + hc:k0 + hc + RET_HEAD_DIM]
            vs = proj[rows, v0 + hc:v0 + hc + RET_HEAD_DIM]
            gs = proj[rows, g0 + hc:g0 + hc + RET_HEAD_DIM]
            qr = qs * c2 + pltpu.roll(qs, RET_HEAD_DIM // 2, 1) * s2
            kr = ks * c2 + pltpu.roll(ks, RET_HEAD_DIM // 2, 1) * s2
            qb = qr.astype(BF16)
            kb = kr.astype(BF16)
            scores = lax.dot_general(qb, kb, (((1,), (1,)), ((), ())), preferred_element_type=F32)
            scores = scores * din_ref[hh]
            inner = _dot(scores.astype(BF16), vs.astype(BF16))
            st = state[hh]
            cross = _dot((qr * xi_ref[hh]).astype(BF16), st.astype(BF16))
            kv = lax.dot_general(kb, (vs * zeta_ref[hh]).astype(BF16), (((0,), (0,)), ((), ())),
                                 preferred_element_type=F32)
            state[hh] = st * cdec_ref[hh] + kv
            o = inner + cross
            mu = jnp.mean(o, axis=-1, keepdims=True)
            oc = o - mu
            var = jnp.mean(oc * oc, axis=-1, keepdims=True)
            o = oc * lax.rsqrt(var + EPS) * gng_ref[:, hc:hc + RET_HEAD_DIM]
            mixin[rows, D_CONV + hc:D_CONV + hc + RET_HEAD_DIM] = (gs * jax.nn.sigmoid(gs) * o).astype(BF16)

    mix = mix_conv + _dot(mixin[:, D_CONV:D_MIX], wout[D_CONV:D_MIX, :])
    o_ref[...] = x_ref[...] + _rms(mix, gpost_ref[...])


def _layer_spec(a, layer):
    zeros = (0,) * (a.ndim - 1)
    return pl.BlockSpec((None,) + a.shape[1:], lambda b, t: (layer,) + zeros, pipeline_mode=pl.Buffered(1))


def _const_spec(a):
    zeros = (0,) * a.ndim
    return pl.BlockSpec(a.shape, lambda b, t: zeros, pipeline_mode=pl.Buffered(1))


def _mixer_call(layer, x, c2, s2, params, tables):
    bsz, seq, _ = x.shape
    tm = TM_MIX
    tile = lambda d: pl.BlockSpec((None, tm, d), lambda b, t: (b, t, 0))
    return pl.pallas_call(
        _mixer_kernel,
        grid=(bsz, seq // tm),
        in_specs=([tile(D_MODEL), tile(RET_HEAD_DIM), tile(RET_HEAD_DIM)]
                  + [_layer_spec(a, layer) for a in params] + [_const_spec(a) for a in tables]),
        out_specs=tile(D_MODEL),
        out_shape=jax.ShapeDtypeStruct(x.shape, x.dtype),
        scratch_shapes=[
            pltpu.VMEM((tm, D_IN_PROJ), F32),
            pltpu.VMEM((SUBLANES, CONV_HALO + tm, D_CONV), F32),
            pltpu.VMEM((tm, D_MIX), BF16),
            pltpu.VMEM((N_RET_HEADS, RET_HEAD_DIM, RET_HEAD_DIM), F32),
            pltpu.VMEM((D_MODEL, D_IN_PROJ), BF16),
            pltpu.VMEM((D_MIX, D_MODEL), BF16),
            pltpu.VMEM((tm, D_CONV), F32),
        ],
        compiler_params=pltpu.CompilerParams(
            dimension_semantics=("arbitrary", "arbitrary"), vmem_limit_bytes=VMEM_LIMIT_BYTES),
        name="mixer",
    )(x, c2, s2, *params, *tables)


def _ffn_kernel(x_ref, p_ref, gpre_ref, gpost_ref, wup_ref, fcw_ref, fcb_ref, wdown_ref, wgate_ref, wple_ref,
                o_ref, hbuf, upv, upg, act):
    tm = x_ref.shape[0]

    @pl.when(pl.program_id(1) == 0)
    def _():
        hbuf[0:FFN_HALO, :] = jnp.zeros((FFN_HALO, D_MODEL), BF16)

    @pl.when(pl.program_id(1) != 0)
    def _():
        hbuf[0:FFN_HALO, :] = hbuf[tm:tm + FFN_HALO, :]

    hbuf[FFN_HALO:FFN_HALO + tm, :] = _rms(x_ref[...], gpre_ref[...]).astype(BF16)

    lead = FFN_HALO - SUBLANES
    for jn, j in enumerate(range(0, D_FF, FFN_COLS)):
        slot = jn % 2
        upv[slot, :, :] = _dot(hbuf[...], wup_ref[:, pl.ds(j, FFN_COLS)])
        upg[slot, :, :] = _dot(hbuf[...], wup_ref[:, pl.ds(D_FF + j, FFN_COLS)])
        for r in range(0, tm, FFN_ROWS):
            conv = []
            for ub, col in ((upv, j), (upg, D_FF + j)):
                cols = pl.ds(col, FFN_COLS)
                a = ub[slot, pl.ds(lead + r, FFN_ROWS + SUBLANES), :]
                s1 = pltpu.roll(a, 1, 0)[SUBLANES:]
                s2 = pltpu.roll(a, 2, 0)[SUBLANES:]
                conv.append(fcb_ref[:, cols] + fcw_ref[2:3, cols] * a[SUBLANES:]
                            + fcw_ref[1:2, cols] * s1 + fcw_ref[0:1, cols] * s2)
            val, gate = conv
            th = jnp.tanh(gate * (GELU_C0 + GELU_C1 * (gate * gate)))
            hg = 0.5 * gate
            act[pl.ds(r, FFN_ROWS), pl.ds(j, FFN_COLS)] = ((hg + hg * th) * val).astype(BF16)

    f = _dot(act[...], wdown_ref[...])
    y = x_ref[...] + _rms(f, gpost_ref[...])
    gate = jax.nn.sigmoid(_dot(y.astype(BF16), wgate_ref[...]))
    o_ref[...] = y + gate * _dot(p_ref[...].astype(BF16), wple_ref[...])


def _ffn_call(layer, x, p, params):
    bsz, seq, _ = x.shape
    tm = TM_FFN
    tile = lambda d: pl.BlockSpec((None, tm, d), lambda b, t: (b, t, 0))
    p_tile = pl.BlockSpec((None, None, tm, PLE_DIM), lambda b, t: (layer, b, t, 0))
    up_block = pltpu.VMEM((2, FFN_HALO + tm, FFN_COLS), F32)
    return pl.pallas_call(
        _ffn_kernel,
        grid=(bsz, seq // tm),
        in_specs=[tile(D_MODEL), p_tile] + [_layer_spec(a, layer) for a in params],
        out_specs=tile(D_MODEL),
        out_shape=jax.ShapeDtypeStruct(x.shape, x.dtype),
        scratch_shapes=[
            pltpu.VMEM((FFN_HALO + tm, D_MODEL), BF16),
            up_block, up_block,
            pltpu.VMEM((tm, D_FF), BF16),
        ],
        compiler_params=pltpu.CompilerParams(
            dimension_semantics=("arbitrary", "arbitrary"), vmem_limit_bytes=VMEM_LIMIT_BYTES),
        name="ffn_ple",
    )(x, p, *params)


def _retention_tables():
    n_h, c = N_RET_HEADS, RET_CHUNK
    log_g = jnp.log1p(-(2.0 ** (-5.0 - jnp.arange(n_h, dtype=F32))))
    idx = jnp.arange(c, dtype=F32)
    rel = idx[:, None] - idx[None, :]
    decay_in = jnp.where(rel >= 0, jnp.exp(log_g[:, None, None] * jnp.maximum(rel, 0.0)), 0.0)
    zeta = jnp.exp(log_g[:, None] * (c - 1 - idx))
    xi = jnp.exp(log_g[:, None] * (idx + 1))
    chunk_decay = jnp.exp(log_g * c)
    lanes = (n_h, c, RET_HEAD_DIM)
    key_scale = RET_HEAD_DIM ** -0.5
    decay_in = decay_in * key_scale
    zeta = zeta * key_scale
    return (decay_in, jnp.broadcast_to(zeta[:, :, None], lanes), jnp.broadcast_to(xi[:, :, None], lanes),
            jnp.broadcast_to(chunk_decay[:, None, None], (n_h, 1, RET_HEAD_DIM)))


def _interleave_glu_columns(w_in):
    depth, d_in, _ = w_in.shape
    n_blk = D_CONV // LANES
    glu = w_in[:, :, :2 * D_CONV].reshape(depth, d_in, 2, n_blk, LANES)
    glu = jnp.swapaxes(glu, 2, 3).reshape(depth, d_in, 2 * D_CONV)
    return jnp.concatenate([glu, w_in[:, :, 2 * D_CONV:]], axis=-1)


def kernel(x, p, positions, norm_mix_pre, norm_mix_post, norm_ffn_pre, norm_ffn_post, w_in, conv_w, conv_b,
           conv_ln_g, conv_ln_b, ret_gn_g, w_out, ffn_up, ffn_conv_w, ffn_conv_b, ffn_down, ple_gate, ple_proj):
    bsz, seq, _ = x.shape
    depth = w_in.shape[0]
    c2, s2 = _rotary_tables(positions)
    c2 = c2.reshape(bsz, seq, RET_HEAD_DIM)
    s2 = s2.reshape(bsz, seq, RET_HEAD_DIM)
    tables = _retention_tables()
    row = lambda a: a[:, None, :]
    mixer_params = (row(norm_mix_pre), row(norm_mix_post), _interleave_glu_columns(w_in).astype(BF16), conv_w,
                    row(conv_b),
                    row(conv_ln_g), row(conv_ln_b), row(ret_gn_g), w_out.astype(BF16))
    ffn_params = (row(norm_ffn_pre), row(norm_ffn_post), ffn_up.astype(BF16), ffn_conv_w, row(ffn_conv_b),
                  ffn_down.astype(BF16), ple_gate.astype(BF16), ple_proj.astype(BF16))
    for layer in range(depth):
        x = _mixer_call(layer, x, c2, s2, mixer_params, tables)
        x = _ffn_call(layer, x, p, ffn_params)
    return x
```

```python
import jax
import jax.numpy as jnp
from jax import lax
from jax.experimental import pallas as pl
from jax.experimental.pallas import tpu as pltpu

D_MODEL = 1024
PLE_DIM = 256
D_CONV = 512
CONV_WIDTH = 31
N_RET_HEADS = 4
RET_HEAD_DIM = 128
D_RET = N_RET_HEADS * RET_HEAD_DIM
D_MIX = D_CONV + D_RET
D_IN_PROJ = 2 * D_CONV + 4 * D_RET
RET_CHUNK = 128
ROPE_BASE = 10000.0
D_FF = 2816
FFN_CONV_WIDTH = 3
EPS = 1e-6

LANES = 128
SUBLANES = 8
VMEM_LIMIT_BYTES = 56 * 1024 * 1024

TM_MIX = 512
TM_FFN = 512
TS_ROT = 1024
CONV_HALO = 32
CONV_ROWS = 64
FFN_COLS = 256
FFN_ROWS = 64
FFN_HALO = SUBLANES
GELU_C0 = 0.7978845608028654
GELU_C1 = 0.044715 * GELU_C0

F32 = jnp.float32
BF16 = jnp.bfloat16


def _rms(x, g):
    ms = jnp.mean(x * x, axis=-1, keepdims=True)
    return x * lax.rsqrt(ms + EPS) * g


def _dot(a, b):
    return jnp.dot(a, b, preferred_element_type=F32)


def _rot_kernel(pos_ref, invf_ref, c2_ref, s2_ref):
    ts = pos_ref.shape[0]
    h = ts // 2
    half = RET_HEAD_DIM // 2
    low = lax.broadcasted_iota(jnp.int32, (h, RET_HEAD_DIM), 1) < half
    pos = pos_ref[...].astype(F32)
    ang = jnp.where(low, pos[0:h], pos[h:ts]) * invf_ref[...]
    c = jnp.cos(ang)
    s = jnp.sin(ang)
    c_sw = pltpu.roll(c, half, 1)
    s_sw = pltpu.roll(s, half, 1)
    c2_ref[0:h, :] = jnp.where(low, c, c_sw)
    c2_ref[h:ts, :] = jnp.where(low, c_sw, c)
    s2_ref[0:h, :] = jnp.where(low, -s, s_sw)
    s2_ref[h:ts, :] = jnp.where(low, -s_sw, s)


def _rotary_tables(positions):
    n = positions.size
    half = RET_HEAD_DIM // 2
    inv_freq = ROPE_BASE ** (-jnp.arange(half, dtype=F32) / half)
    invf2 = jnp.concatenate([inv_freq, inv_freq])[None, :]
    row = pl.BlockSpec((1, RET_HEAD_DIM), lambda i: (0, 0))
    tab = pl.BlockSpec((TS_ROT, RET_HEAD_DIM), lambda i: (i, 0))
    return pl.pallas_call(
        _rot_kernel,
        grid=(n // TS_ROT,),
        in_specs=[pl.BlockSpec((TS_ROT, 1), lambda i: (i, 0)), row],
        out_specs=[tab, tab],
        out_shape=[jax.ShapeDtypeStruct((n, RET_HEAD_DIM), F32)] * 2,
        name="rotary_tables",
    )(positions.reshape(n, 1), invf2)


def _mixer_kernel(x_ref, c2_ref, s2_ref, gpre_ref, gpost_ref, win_ref, cw_ref, cb_ref, lng_ref, lnb_ref,
                  gng_ref, wout_ref, din_ref, zeta_ref, xi_ref, cdec_ref, o_ref,
                  proj, ush, mixin, state, win, wout, ybuf):
    tm = x_ref.shape[0]
    n_u = CONV_HALO + tm

    @pl.when(pl.program_id(1) == 0)
    def _():
        ush[0, 0:CONV_HALO, :] = jnp.zeros((CONV_HALO, D_CONV), F32)
        state[...] = jnp.zeros_like(state)

    @pl.when((pl.program_id(0) == 0) & (pl.program_id(1) == 0))
    def _():
        for c in range(0, D_CONV, LANES):
            win[:, 2 * c:2 * c + LANES] = win_ref[:, c:c + LANES]
            win[:, 2 * c + LANES:2 * c + 2 * LANES] = win_ref[:, D_CONV + c:D_CONV + c + LANES]
        win[:, 2 * D_CONV:] = win_ref[:, 2 * D_CONV:]
        wout[...] = wout_ref[...]

    h = _rms(x_ref[...], gpre_ref[...]).astype(BF16)
    proj[...] = _dot(h, win[...])

    first = CONV_HALO - (CONV_WIDTH - 1)
    for c in range(0, D_CONV, LANES):
        cols = pl.ds(c, LANES)
        for r in range(0, tm, CONV_ROWS):
            rows = pl.ds(r, CONV_ROWS)
            ush[0, pl.ds(CONV_HALO + r, CONV_ROWS), cols] = (
                proj[rows, 2 * c:2 * c + LANES] * jax.nn.sigmoid(proj[rows, 2 * c + LANES:2 * c + 2 * LANES]))
        u = ush[0, :, cols]
        for s in range(1, SUBLANES):
            ush[s, :, cols] = pltpu.roll(u, n_u - s, 0)
        for r in range(0, tm, CONV_ROWS):
            acc = jnp.broadcast_to(cb_ref[:, cols], (CONV_ROWS, LANES))
            for k in range(CONV_WIDTH):
                off = first + k
                acc = acc + cw_ref[k:k + 1, cols] * ush[off % SUBLANES,
                                                         pl.ds(r + off - off % SUBLANES, CONV_ROWS), cols]
            ybuf[pl.ds(r, CONV_ROWS), cols] = acc
    for r in range(0, tm, CONV_ROWS):
        y = ybuf[pl.ds(r, CONV_ROWS), :]
        mu = jnp.mean(y, axis=-1, keepdims=True)
        yc = y - mu
        var = jnp.mean(yc * yc, axis=-1, keepdims=True)
        y = yc * lax.rsqrt(var + EPS) * lng_ref[...] + lnb_ref[...]
        mixin[pl.ds(r, CONV_ROWS), 0:D_CONV] = (y * jax.nn.sigmoid(y)).astype(BF16)
    ush[0, 0:CONV_HALO, :] = ush[0, tm:tm + CONV_HALO, :]
    mix_conv = _dot(mixin[:, 0:D_CONV], wout[0:D_CONV, :])

    q0, k0, v0, g0 = 2 * D_CONV, 2 * D_CONV + D_RET, 2 * D_CONV + 2 * D_RET, 2 * D_CONV + 3 * D_RET
    for r in range(0, tm, RET_CHUNK):
        rows = pl.ds(r, RET_CHUNK)
        c2 = c2_ref[rows, :]
        s2 = s2_ref[rows, :]
        for hh in range(N_RET_HEADS):
            hc = hh * RET_HEAD_DIM
            qs = proj[rows, q0 + hc:q0 + hc + RET_HEAD_DIM]
            ks = proj[rows, k0 + hc:k0 + hc + RET_HEAD_DIM]
            vs = proj[rows, v0 + hc:v0 + hc + RET_HEAD_DIM]
            gs = proj[rows, g0 + hc:g0 + hc + RET_HEAD_DIM]
            qr = qs * c2 + pltpu.roll(qs, RET_HEAD_DIM // 2, 1) * s2
            kr = ks * c2 + pltpu.roll(ks, RET_HEAD_DIM // 2, 1) * s2
            qb = qr.astype(BF16)
            kb = kr.astype(BF16)
            scores = lax.dot_general(qb, kb, (((1,), (1,)), ((), ())), preferred_element_type=F32)
            scores = scores * din_ref[hh]
            inner = _dot(scores.astype(BF16), vs.astype(BF16))
            st = state[hh]
            cross = _dot((qr * xi_ref[hh]).astype(BF16), st.astype(BF16))
            kv = lax.dot_general(kb, (vs * zeta_ref[hh]).astype(BF16), (((0,), (0,)), ((), ())),
                                 preferred_element_type=F32)
            state[hh] = st * cdec_ref[hh] + kv
            o = inner + cross
            mu = jnp.mean(o, axis=-1, keepdims=True)
            oc = o - mu
            var = jnp.mean(oc * oc, axis=-1, keepdims=True)
            o = oc * lax.rsqrt(var + EPS) * gng_ref[:, hc:hc + RET_HEAD_DIM]
            mixin[rows, D_CONV + hc:D_CONV + hc + RET_HEAD_DIM] = (gs * jax.nn.sigmoid(gs) * o).astype(BF16)

    mix = mix_conv + _dot(mixin[:, D_CONV:D_MIX], wout[D_CONV:D_MIX, :])
    o_ref[...] = x_ref[...] + _rms(mix, gpost_ref[...])


def _layer_spec(a, layer):
    zeros = (0,) * (a.ndim - 1)
    return pl.BlockSpec((None,) + a.shape[1:], lambda b, t: (layer,) + zeros, pipeline_mode=pl.Buffered(1))


def _const_spec(a):
    zeros = (0,) * a.ndim
    return pl.BlockSpec(a.shape, lambda b, t: zeros, pipeline_mode=pl.Buffered(1))


def _mixer_call(layer, x, c2, s2, params, tables):
    bsz, seq, _ = x.shape
    tm = TM_MIX
    tile = lambda d: pl.BlockSpec((None, tm, d), lambda b, t: (b, t, 0))
    return pl.pallas_call(
        _mixer_kernel,
        grid=(bsz, seq // tm),
        in_specs=([tile(D_MODEL), tile(RET_HEAD_DIM), tile(RET_HEAD_DIM)]
                  + [_layer_spec(a, layer) for a in params] + [_const_spec(a) for a in tables]),
        out_specs=tile(D_MODEL),
        out_shape=jax.ShapeDtypeStruct(x.shape, x.dtype),
        scratch_shapes=[
            pltpu.VMEM((tm, D_IN_PROJ), F32),
            pltpu.VMEM((SUBLANES, CONV_HALO + tm, D_CONV), F32),
            pltpu.VMEM((tm, D_MIX), BF16),
            pltpu.VMEM((N_RET_HEADS, RET_HEAD_DIM, RET_HEAD_DIM), F32),
            pltpu.VMEM((D_MODEL, D_IN_PROJ), BF16),
            pltpu.VMEM((D_MIX, D_MODEL), BF16),
            pltpu.VMEM((tm, D_CONV), F32),
        ],
        compiler_params=pltpu.CompilerParams(
            dimension_semantics=("arbitrary", "arbitrary"), vmem_limit_bytes=VMEM_LIMIT_BYTES),
        name="mixer",
    )(x, c2, s2, *params, *tables)


def _ffn_kernel(x_ref, p_ref, gpre_ref, gpost_ref, wup_ref, fcw_ref, fcb_ref, wdown_ref, wgate_ref, wple_ref,
                o_ref, hbuf, upv, upg, act, hist):
    tm = x_ref.shape[0]

    @pl.when(pl.program_id(1) == 0)
    def _():
        hist[...] = jnp.zeros_like(hist)

    hbuf[...] = _rms(x_ref[...], gpre_ref[...]).astype(BF16)

    for jn, j in enumerate(range(0, D_FF, FFN_COLS)):
        slot = jn % 2
        for ub, col in ((upv, j), (upg, D_FF + j)):
            cols = pl.ds(col, FFN_COLS)
            ub[slot, 0:FFN_HALO, :] = hist[:, cols]
            ub[slot, FFN_HALO:FFN_HALO + tm, :] = _dot(hbuf[...], wup_ref[:, cols])
            hist[:, cols] = ub[slot, tm:tm + FFN_HALO, :]
        for r in range(0, tm, FFN_ROWS):
            conv = []
            for ub, col in ((upv, j), (upg, D_FF + j)):
                cols = pl.ds(col, FFN_COLS)
                a = ub[slot, pl.ds(r, FFN_ROWS + SUBLANES), :]
                s1 = pltpu.roll(a, 1, 0)[SUBLANES:]
                s2 = pltpu.roll(a, 2, 0)[SUBLANES:]
                conv.append(fcb_ref[:, cols] + fcw_ref[2:3, cols] * a[SUBLANES:]
                            + fcw_ref[1:2, cols] * s1 + fcw_ref[0:1, cols] * s2)
            val, gate = conv
            th = jnp.tanh(gate * (GELU_C0 + GELU_C1 * (gate * gate)))
            hg = 0.5 * gate
            act[pl.ds(r, FFN_ROWS), pl.ds(j, FFN_COLS)] = ((hg + hg * th) * val).astype(BF16)

    f = _dot(act[...], wdown_ref[...])
    y = x_ref[...] + _rms(f, gpost_ref[...])
    gate = jax.nn.sigmoid(_dot(y.astype(BF16), wgate_ref[...]))
    o_ref[...] = y + gate * _dot(p_ref[...].astype(BF16), wple_ref[...])


def _ffn_call(layer, x, p, params):
    bsz, seq, _ = x.shape
    tm = TM_FFN
    tile = lambda d: pl.BlockSpec((None, tm, d), lambda b, t: (b, t, 0))
    p_tile = pl.BlockSpec((None, None, tm, PLE_DIM), lambda b, t: (layer, b, t, 0))
    up_block = pltpu.VMEM((2, FFN_HALO + tm, FFN_COLS), F32)
    return pl.pallas_call(
        _ffn_kernel,
        grid=(bsz, seq // tm),
        in_specs=[tile(D_MODEL), p_tile] + [_layer_spec(a, layer) for a in params],
        out_specs=tile(D_MODEL),
        out_shape=jax.ShapeDtypeStruct(x.shape, x.dtype),
        scratch_shapes=[
            pltpu.VMEM((tm, D_MODEL), BF16),
            up_block, up_block,
            pltpu.VMEM((tm, D_FF), BF16),
            pltpu.VMEM((FFN_HALO, 2 * D_FF), F32),
        ],
        compiler_params=pltpu.CompilerParams(
            dimension_semantics=("arbitrary", "arbitrary"), vmem_limit_bytes=VMEM_LIMIT_BYTES),
        name="ffn_ple",
    )(x, p, *params)


def _retention_tables():
    n_h, c = N_RET_HEADS, RET_CHUNK
    log_g = jnp.log1p(-(2.0 ** (-5.0 - jnp.arange(n_h, dtype=F32))))
    idx = jnp.arange(c, dtype=F32)
    rel = idx[:, None] - idx[None, :]
    decay_in = jnp.where(rel >= 0, jnp.exp(log_g[:, None, None] * jnp.maximum(rel, 0.0)), 0.0)
    zeta = jnp.exp(log_g[:, None] * (c - 1 - idx))
    xi = jnp.exp(log_g[:, None] * (idx + 1))
    chunk_decay = jnp.exp(log_g * c)
    lanes = (n_h, c, RET_HEAD_DIM)
    key_scale = RET_HEAD_DIM ** -0.5
    decay_in = decay_in * key_scale
    zeta = zeta * key_scale
    return (decay_in, jnp.broadcast_to(zeta[:, :, None], lanes), jnp.broadcast_to(xi[:, :, None], lanes),
            jnp.broadcast_to(chunk_decay[:, None, None], (n_h, 1, RET_HEAD_DIM)))


def kernel(x, p, positions, norm_mix_pre, norm_mix_post, norm_ffn_pre, norm_ffn_post, w_in, conv_w, conv_b,
           conv_ln_g, conv_ln_b, ret_gn_g, w_out, ffn_up, ffn_conv_w, ffn_conv_b, ffn_down, ple_gate, ple_proj):
    bsz, seq, _ = x.shape
    depth = w_in.shape[0]
    c2, s2 = _rotary_tables(positions)
    c2 = c2.reshape(bsz, seq, RET_HEAD_DIM)
    s2 = s2.reshape(bsz, seq, RET_HEAD_DIM)
    tables = _retention_tables()
    row = lambda a: a[:, None, :]
    mixer_params = (row(norm_mix_pre), row(norm_mix_post), w_in.astype(BF16), conv_w, row(conv_b),
                    row(conv_ln_g), row(conv_ln_b), row(ret_gn_g), w_out.astype(BF16))
    ffn_params = (row(norm_ffn_pre), row(norm_ffn_post), ffn_up.astype(BF16), ffn_conv_w, row(ffn_conv_b),
                  ffn_down.astype(BF16), ple_gate.astype(BF16), ple_proj.astype(BF16))
    for layer in range(depth):
        x = _mixer_call(layer, x, c2, s2, mixer_params, tables)
        x = _ffn_call(layer, x, p, ffn_params)
    return x
```

```python
import jax
import jax.numpy as jnp
from jax import lax
from jax.experimental import pallas as pl
from jax.experimental.pallas import tpu as pltpu

D_MODEL = 1024
PLE_DIM = 256
D_CONV = 512
CONV_WIDTH = 31
N_RET_HEADS = 4
RET_HEAD_DIM = 128
D_RET = N_RET_HEADS * RET_HEAD_DIM
D_MIX = D_CONV + D_RET
D_IN_PROJ = 2 * D_CONV + 4 * D_RET
RET_CHUNK = 128
ROPE_BASE = 10000.0
D_FF = 2816
FFN_CONV_WIDTH = 3
EPS = 1e-6

LANES = 128
SUBLANES = 8
VMEM_LIMIT_BYTES = 56 * 1024 * 1024

TM_MIX = 512
TM_FFN = 512
TS_ROT = 2048
CONV_HALO = 32
CONV_ROWS = 64
FFN_COLS = 256
FFN_ROWS = 256
FFN_HALO = SUBLANES
GELU_C0 = 0.7978845608028654
GELU_C1 = 0.044715 * GELU_C0

F32 = jnp.float32
BF16 = jnp.bfloat16


def _rms(x, g):
    ms = jnp.mean(x * x, axis=-1, keepdims=True)
    return x * lax.rsqrt(ms + EPS) * g


def _dot(a, b):
    return jnp.dot(a, b, preferred_element_type=F32)


def _rot_kernel(pos_ref, invf_ref, c2_ref, s2_ref):
    ts = pos_ref.shape[0]
    h = ts // 2
    half = RET_HEAD_DIM // 2
    low = lax.broadcasted_iota(jnp.int32, (h, RET_HEAD_DIM), 1) < half
    pos = pos_ref[...].astype(F32)
    ang = jnp.where(low, pos[0:h], pos[h:ts]) * invf_ref[...]
    c = jnp.cos(ang)
    s = jnp.sin(ang)
    c_sw = pltpu.roll(c, half, 1)
    s_sw = pltpu.roll(s, half, 1)
    c2_ref[0:h, :] = jnp.where(low, c, c_sw)
    c2_ref[h:ts, :] = jnp.where(low, c_sw, c)
    s2_ref[0:h, :] = jnp.where(low, -s, s_sw)
    s2_ref[h:ts, :] = jnp.where(low, -s_sw, s)


def _rotary_tables(positions):
    n = positions.size
    half = RET_HEAD_DIM // 2
    inv_freq = ROPE_BASE ** (-jnp.arange(half, dtype=F32) / half)
    invf2 = jnp.concatenate([inv_freq, inv_freq])[None, :]
    row = pl.BlockSpec((1, RET_HEAD_DIM), lambda i: (0, 0))
    tab = pl.BlockSpec((TS_ROT, RET_HEAD_DIM), lambda i: (i, 0))
    return pl.pallas_call(
        _rot_kernel,
        grid=(n // TS_ROT,),
        in_specs=[pl.BlockSpec((TS_ROT, 1), lambda i: (i, 0)), row],
        out_specs=[tab, tab],
        out_shape=[jax.ShapeDtypeStruct((n, RET_HEAD_DIM), F32)] * 2,
        name="rotary_tables",
    )(positions.reshape(n, 1), invf2)


def _mixer_kernel(x_ref, c2_ref, s2_ref, gpre_ref, gpost_ref, win_ref, cw_ref, cb_ref, lng_ref, lnb_ref,
                  gng_ref, wout_ref, din_ref, zeta_ref, xi_ref, cdec_ref, o_ref,
                  proj, ush, mixin, state, win, wout, ybuf):
    tm = x_ref.shape[0]
    n_u = CONV_HALO + tm

    @pl.when(pl.program_id(1) == 0)
    def _():
        ush[0, 0:CONV_HALO, :] = jnp.zeros((CONV_HALO, D_CONV), F32)
        state[...] = jnp.zeros_like(state)

    @pl.when((pl.program_id(0) == 0) & (pl.program_id(1) == 0))
    def _():
        for c in range(0, D_CONV, LANES):
            win[:, 2 * c:2 * c + LANES] = win_ref[:, c:c + LANES]
            win[:, 2 * c + LANES:2 * c + 2 * LANES] = win_ref[:, D_CONV + c:D_CONV + c + LANES]
        win[:, 2 * D_CONV:] = win_ref[:, 2 * D_CONV:]
        wout[...] = wout_ref[...]

    h = _rms(x_ref[...], gpre_ref[...]).astype(BF16)
    proj[...] = _dot(h, win[...])

    first = CONV_HALO - (CONV_WIDTH - 1)
    for c in range(0, D_CONV, LANES):
        cols = pl.ds(c, LANES)
        for r in range(0, tm, CONV_ROWS):
            rows = pl.ds(r, CONV_ROWS)
            ush[0, pl.ds(CONV_HALO + r, CONV_ROWS), cols] = (
                proj[rows, 2 * c:2 * c + LANES] * jax.nn.sigmoid(proj[rows, 2 * c + LANES:2 * c + 2 * LANES]))
        u = ush[0, :, cols]
        for s in range(1, SUBLANES):
            ush[s, :, cols] = pltpu.roll(u, n_u - s, 0)
        for r in range(0, tm, CONV_ROWS):
            acc = jnp.broadcast_to(cb_ref[:, cols], (CONV_ROWS, LANES))
            for k in range(CONV_WIDTH):
                off = first + k
                acc = acc + cw_ref[k:k + 1, cols] * ush[off % SUBLANES,
                                                         pl.ds(r + off - off % SUBLANES, CONV_ROWS), cols]
            ybuf[pl.ds(r, CONV_ROWS), cols] = acc
    for r in range(0, tm, CONV_ROWS):
        y = ybuf[pl.ds(r, CONV_ROWS), :]
        mu = jnp.mean(y, axis=-1, keepdims=True)
        yc = y - mu
        var = jnp.mean(yc * yc, axis=-1, keepdims=True)
        y = yc * lax.rsqrt(var + EPS) * lng_ref[...] + lnb_ref[...]
        mixin[pl.ds(r, CONV_ROWS), 0:D_CONV] = (y * jax.nn.sigmoid(y)).astype(BF16)
    ush[0, 0:CONV_HALO, :] = ush[0, tm:tm + CONV_HALO, :]
    mix_conv = _dot(mixin[:, 0:D_CONV], wout[0:D_CONV, :])

    q0, k0, v0, g0 = 2 * D_CONV, 2 * D_CONV + D_RET, 2 * D_CONV + 2 * D_RET, 2 * D_CONV + 3 * D_RET
    for r in range(0, tm, RET_CHUNK):
        rows = pl.ds(r, RET_CHUNK)
        c2 = c2_ref[rows, :]
        s2 = s2_ref[rows, :]
        for hh in range(N_RET_HEADS):
            hc = hh * RET_HEAD_DIM
            qs = proj[rows, q0 + hc:q0 + hc + RET_HEAD_DIM]
            ks = proj[rows, k0 + hc:k0 + hc + RET_HEAD_DIM]
            vs = proj[rows, v0 + hc:v0 + hc + RET_HEAD_DIM]
            gs = proj[rows, g0 + hc:g0 + hc + RET_HEAD_DIM]
            qr = qs * c2 + pltpu.roll(qs, RET_HEAD_DIM // 2, 1) * s2
            kr = ks * c2 + pltpu.roll(ks, RET_HEAD_DIM // 2, 1) * s2
            qb = qr.astype(BF16)
            kb = kr.astype(BF16)
            scores = lax.dot_general(qb, kb, (((1,), (1,)), ((), ())), preferred_element_type=F32)
            scores = scores * din_ref[hh]
            inner = _dot(scores.astype(BF16), vs.astype(BF16))
            st = state[hh]
            cross = _dot((qr * xi_ref[hh]).astype(BF16), st.astype(BF16))
            kv = lax.dot_general(kb, (vs * zeta_ref[hh]).astype(BF16), (((0,), (0,)), ((), ())),
                                 preferred_element_type=F32)
            state[hh] = st * cdec_ref[hh] + kv
            o = inner + cross
            mu = jnp.mean(o, axis=-1, keepdims=True)
            oc = o - mu
            var = jnp.mean(oc * oc, axis=-1, keepdims=True)
            o = oc * lax.rsqrt(var + EPS) * gng_ref[:, hc:hc + RET_HEAD_DIM]
            mixin[rows, D_CONV + hc:D_CONV + hc + RET_HEAD_DIM] = (gs * jax.nn.sigmoid(gs) * o).astype(BF16)

    mix = mix_conv + _dot(mixin[:, D_CONV:D_MIX], wout[D_CONV:D_MIX, :])
    o_ref[...] = x_ref[...] + _rms(mix, gpost_ref[...])


def _layer_spec(a, layer):
    zeros = (0,) * (a.ndim - 1)
    return pl.BlockSpec((None,) + a.shape[1:], lambda b, t: (layer,) + zeros, pipeline_mode=pl.Buffered(1))


def _const_spec(a):
    zeros = (0,) * a.ndim
    return pl.BlockSpec(a.shape, lambda b, t: zeros, pipeline_mode=pl.Buffered(1))


def _mixer_call(layer, x, c2, s2, params, tables):
    bsz, seq, _ = x.shape
    tm = TM_MIX
    tile = lambda d: pl.BlockSpec((None, tm, d), lambda b, t: (b, t, 0))
    return pl.pallas_call(
        _mixer_kernel,
        grid=(bsz, seq // tm),
        in_specs=([tile(D_MODEL), tile(RET_HEAD_DIM), tile(RET_HEAD_DIM)]
                  + [_layer_spec(a, layer) for a in params] + [_const_spec(a) for a in tables]),
        out_specs=tile(D_MODEL),
        out_shape=jax.ShapeDtypeStruct(x.shape, x.dtype),
        scratch_shapes=[
            pltpu.VMEM((tm, D_IN_PROJ), F32),
            pltpu.VMEM((SUBLANES, CONV_HALO + tm, D_CONV), F32),
            pltpu.VMEM((tm, D_MIX), BF16),
            pltpu.VMEM((N_RET_HEADS, RET_HEAD_DIM, RET_HEAD_DIM), F32),
            pltpu.VMEM((D_MODEL, D_IN_PROJ), BF16),
            pltpu.VMEM((D_MIX, D_MODEL), BF16),
            pltpu.VMEM((tm, D_CONV), F32),
        ],
        compiler_params=pltpu.CompilerParams(
            dimension_semantics=("arbitrary", "arbitrary"), vmem_limit_bytes=VMEM_LIMIT_BYTES),
        name="mixer",
    )(x, c2, s2, *params, *tables)


def _ffn_kernel(x_ref, p_ref, gpre_ref, gpost_ref, wup_ref, fcw_ref, fcb_ref, wdown_ref, wgate_ref, wple_ref,
                o_ref, hbuf, upv, upg, act, hist, wup, wdown, wgate, wple):
    tm = x_ref.shape[0]

    @pl.when((pl.program_id(0) == 0) & (pl.program_id(1) == 0))
    def _():
        for c in range(0, 2 * D_FF, FFN_COLS):
            wup[:, c:c + FFN_COLS] = wup_ref[:, c:c + FFN_COLS]
        for c in range(0, D_FF, FFN_COLS):
            wdown[c:c + FFN_COLS, :] = wdown_ref[c:c + FFN_COLS, :]
        wgate[...] = wgate_ref[...]
        wple[...] = wple_ref[...]

    @pl.when(pl.program_id(1) == 0)
    def _():
        hist[...] = jnp.zeros_like(hist)

    hbuf[...] = _rms(x_ref[...], gpre_ref[...]).astype(BF16)

    for jn, j in enumerate(range(0, D_FF, FFN_COLS)):
        slot = jn % 2
        for ub, col in ((upv, j), (upg, D_FF + j)):
            cols = pl.ds(col, FFN_COLS)
            ub[slot, 0:FFN_HALO, :] = hist[:, cols]
            ub[slot, FFN_HALO:FFN_HALO + tm, :] = _dot(hbuf[...], wup[:, cols])
            hist[:, cols] = ub[slot, tm:tm + FFN_HALO, :]
        for r in range(0, tm, FFN_ROWS):
            conv = []
            for ub, col in ((upv, j), (upg, D_FF + j)):
                cols = pl.ds(col, FFN_COLS)
                a = ub[slot, pl.ds(r, FFN_ROWS + SUBLANES), :]
                s1 = pltpu.roll(a, 1, 0)[SUBLANES:]
                s2 = pltpu.roll(a, 2, 0)[SUBLANES:]
                conv.append(fcb_ref[:, cols] + fcw_ref[2:3, cols] * a[SUBLANES:]
                            + fcw_ref[1:2, cols] * s1 + fcw_ref[0:1, cols] * s2)
            val, gate = conv
            th = jnp.tanh(gate * (GELU_C0 + GELU_C1 * (gate * gate)))
            hg = 0.5 * gate
            act[pl.ds(r, FFN_ROWS), pl.ds(j, FFN_COLS)] = ((hg + hg * th) * val).astype(BF16)

    f = _dot(act[...], wdown[...])
    y = x_ref[...] + _rms(f, gpost_ref[...])
    gate = jax.nn.sigmoid(_dot(y.astype(BF16), wgate[...]))
    o_ref[...] = y + gate * _dot(p_ref[...].astype(BF16), wple[...])


def _ffn_call(layer, x, p, params):
    bsz, seq, _ = x.shape
    tm = TM_FFN
    tile = lambda d: pl.BlockSpec((None, tm, d), lambda b, t: (b, t, 0))
    p_tile = pl.BlockSpec((None, None, tm, PLE_DIM), lambda b, t: (layer, b, t, 0))
    up_block = pltpu.VMEM((2, FFN_HALO + tm, FFN_COLS), F32)
    return pl.pallas_call(
        _ffn_kernel,
        grid=(bsz, seq // tm),
        in_specs=[tile(D_MODEL), p_tile] + [_layer_spec(a, layer) for a in params],
        out_specs=tile(D_MODEL),
        out_shape=jax.ShapeDtypeStruct(x.shape, x.dtype),
        scratch_shapes=[
            pltpu.VMEM((tm, D_MODEL), BF16),
            up_block, up_block,
            pltpu.VMEM((tm, D_FF), BF16),
            pltpu.VMEM((FFN_HALO, 2 * D_FF), F32),
            pltpu.VMEM((D_MODEL, 2 * D_FF), BF16),
            pltpu.VMEM((D_FF, D_MODEL), BF16),
            pltpu.VMEM((D_MODEL, D_MODEL), BF16),
            pltpu.VMEM((PLE_DIM, D_MODEL), BF16),
        ],
        compiler_params=pltpu.CompilerParams(
            dimension_semantics=("arbitrary", "arbitrary"), vmem_limit_bytes=VMEM_LIMIT_BYTES),
        name="ffn_ple",
    )(x, p, *params)


def _retention_tables():
    n_h, c = N_RET_HEADS, RET_CHUNK
    log_g = jnp.log1p(-(2.0 ** (-5.0 - jnp.arange(n_h, dtype=F32))))
    idx = jnp.arange(c, dtype=F32)
    rel = idx[:, None] - idx[None, :]
    decay_in = jnp.where(rel >= 0, jnp.exp(log_g[:, None, None] * jnp.maximum(rel, 0.0)), 0.0)
    zeta = jnp.exp(log_g[:, None] * (c - 1 - idx))
    xi = jnp.exp(log_g[:, None] * (idx + 1))
    chunk_decay = jnp.exp(log_g * c)
    lanes = (n_h, c, RET_HEAD_DIM)
    key_scale = RET_HEAD_DIM ** -0.5
    decay_in = decay_in * key_scale
    zeta = zeta * key_scale
    return (decay_in, jnp.broadcast_to(zeta[:, :, None], lanes), jnp.broadcast_to(xi[:, :, None], lanes),
            jnp.broadcast_to(chunk_decay[:, None, None], (n_h, 1, RET_HEAD_DIM)))


def kernel(x, p, positions, norm_mix_pre, norm_mix_post, norm_ffn_pre, norm_ffn_post, w_in, conv_w, conv_b,
           conv_ln_g, conv_ln_b, ret_gn_g, w_out, ffn_up, ffn_conv_w, ffn_conv_b, ffn_down, ple_gate, ple_proj):
    bsz, seq, _ = x.shape
    depth = w_in.shape[0]
    c2, s2 = _rotary_tables(positions)
    c2 = c2.reshape(bsz, seq, RET_HEAD_DIM)
    s2 = s2.reshape(bsz, seq, RET_HEAD_DIM)
    tables = _retention_tables()
    row = lambda a: a[:, None, :]
    mixer_params = (row(norm_mix_pre), row(norm_mix_post), w_in.astype(BF16), conv_w, row(conv_b),
                    row(conv_ln_g), row(conv_ln_b), row(ret_gn_g), w_out.astype(BF16))
    ffn_params = (row(norm_ffn_pre), row(norm_ffn_post), ffn_up.astype(BF16), ffn_conv_w, row(ffn_conv_b),
                  ffn_down.astype(BF16), ple_gate.astype(BF16), ple_proj.astype(BF16))
    for layer in range(depth):
        x = _mixer_call(layer, x, c2, s2, mixer_params, tables)
        x = _ffn_call(layer, x, p, ffn_params)
    return x
```

```python
import jax
import jax.numpy as jnp
from jax import lax
from jax.experimental import pallas as pl
from jax.experimental.pallas import tpu as pltpu

D_MODEL = 1024
PLE_DIM = 256
D_CONV = 512
CONV_WIDTH = 31
N_RET_HEADS = 4
RET_HEAD_DIM = 128
D_RET = N_RET_HEADS * RET_HEAD_DIM
D_MIX = D_CONV + D_RET
D_IN_PROJ = 2 * D_CONV + 4 * D_RET
RET_CHUNK = 128
ROPE_BASE = 10000.0
D_FF = 2816
FFN_CONV_WIDTH = 3
EPS = 1e-6

LANES = 128
SUBLANES = 8
VMEM_LIMIT_BYTES = 56 * 1024 * 1024

TM_MIX = 512
TM_FFN = 512
TS_ROT = 2048
CONV_HALO = 32
CONV_ROWS = 64
FFN_COLS = 256
FFN_ROWS = 512
FFN_HALO = SUBLANES
GELU_C0 = 0.7978845608028654
GELU_C1 = 0.044715 * GELU_C0

F32 = jnp.float32
BF16 = jnp.bfloat16


def _rms(x, g):
    ms = jnp.mean(x * x, axis=-1, keepdims=True)
    return x * lax.rsqrt(ms + EPS) * g


def _dot(a, b):
    return jnp.dot(a, b, preferred_element_type=F32)


def _rot_kernel(pos_ref, invf_ref, c2_ref, s2_ref):
    ts = pos_ref.shape[0]
    h = ts // 2
    half = RET_HEAD_DIM // 2
    low = lax.broadcasted_iota(jnp.int32, (h, RET_HEAD_DIM), 1) < half
    pos = pos_ref[...].astype(F32)
    ang = jnp.where(low, pos[0:h], pos[h:ts]) * invf_ref[...]
    c = jnp.cos(ang)
    s = jnp.sin(ang)
    c_sw = pltpu.roll(c, half, 1)
    s_sw = pltpu.roll(s, half, 1)
    c2_ref[0:h, :] = jnp.where(low, c, c_sw)
    c2_ref[h:ts, :] = jnp.where(low, c_sw, c)
    s2_ref[0:h, :] = jnp.where(low, -s, s_sw)
    s2_ref[h:ts, :] = jnp.where(low, -s_sw, s)


def _rotary_tables(positions):
    n = positions.size
    half = RET_HEAD_DIM // 2
    inv_freq = ROPE_BASE ** (-jnp.arange(half, dtype=F32) / half)
    invf2 = jnp.concatenate([inv_freq, inv_freq])[None, :]
    row = pl.BlockSpec((1, RET_HEAD_DIM), lambda i: (0, 0))
    tab = pl.BlockSpec((TS_ROT, RET_HEAD_DIM), lambda i: (i, 0))
    return pl.pallas_call(
        _rot_kernel,
        grid=(n // TS_ROT,),
        in_specs=[pl.BlockSpec((TS_ROT, 1), lambda i: (i, 0)), row],
        out_specs=[tab, tab],
        out_shape=[jax.ShapeDtypeStruct((n, RET_HEAD_DIM), F32)] * 2,
        name="rotary_tables",
    )(positions.reshape(n, 1), invf2)


def _mixer_kernel(x_ref, c2_ref, s2_ref, gpre_ref, gpost_ref, win_ref, cw_ref, cb_ref, lng_ref, lnb_ref,
                  gng_ref, wout_ref, din_ref, zeta_ref, xi_ref, cdec_ref, o_ref,
                  proj, ush, mixin, state, win, wout, ybuf):
    tm = x_ref.shape[0]
    n_u = CONV_HALO + tm

    @pl.when(pl.program_id(1) == 0)
    def _():
        ush[0, 0:CONV_HALO, :] = jnp.zeros((CONV_HALO, D_CONV), F32)
        state[...] = jnp.zeros_like(state)

    @pl.when((pl.program_id(0) == 0) & (pl.program_id(1) == 0))
    def _():
        for c in range(0, D_CONV, LANES):
            win[:, 2 * c:2 * c + LANES] = win_ref[:, c:c + LANES].astype(BF16)
            win[:, 2 * c + LANES:2 * c + 2 * LANES] = win_ref[:, D_CONV + c:D_CONV + c + LANES].astype(BF16)
        for c in range(2 * D_CONV, D_IN_PROJ, 2 * D_CONV):
            win[:, c:c + 2 * D_CONV] = win_ref[:, c:c + 2 * D_CONV].astype(BF16)
        wout[...] = wout_ref[...].astype(BF16)

    h = _rms(x_ref[...], gpre_ref[...]).astype(BF16)
    proj[...] = _dot(h, win[...])

    first = CONV_HALO - (CONV_WIDTH - 1)
    for c in range(0, D_CONV, LANES):
        cols = pl.ds(c, LANES)
        for r in range(0, tm, CONV_ROWS):
            rows = pl.ds(r, CONV_ROWS)
            ush[0, pl.ds(CONV_HALO + r, CONV_ROWS), cols] = (
                proj[rows, 2 * c:2 * c + LANES] * jax.nn.sigmoid(proj[rows, 2 * c + LANES:2 * c + 2 * LANES]))
        u = ush[0, :, cols]
        for s in range(1, SUBLANES):
            ush[s, :, cols] = pltpu.roll(u, n_u - s, 0)
        for r in range(0, tm, CONV_ROWS):
            acc = jnp.broadcast_to(cb_ref[:, cols], (CONV_ROWS, LANES))
            for k in range(CONV_WIDTH):
                off = first + k
                acc = acc + cw_ref[k:k + 1, cols] * ush[off % SUBLANES,
                                                         pl.ds(r + off - off % SUBLANES, CONV_ROWS), cols]
            ybuf[pl.ds(r, CONV_ROWS), cols] = acc
    for r in range(0, tm, CONV_ROWS):
        y = ybuf[pl.ds(r, CONV_ROWS), :]
        mu = jnp.mean(y, axis=-1, keepdims=True)
        yc = y - mu
        var = jnp.mean(yc * yc, axis=-1, keepdims=True)
        y = yc * lax.rsqrt(var + EPS) * lng_ref[...] + lnb_ref[...]
        mixin[pl.ds(r, CONV_ROWS), 0:D_CONV] = (y * jax.nn.sigmoid(y)).astype(BF16)
    ush[0, 0:CONV_HALO, :] = ush[0, tm:tm + CONV_HALO, :]
    mix_conv = _dot(mixin[:, 0:D_CONV], wout[0:D_CONV, :])

    q0, k0, v0, g0 = 2 * D_CONV, 2 * D_CONV + D_RET, 2 * D_CONV + 2 * D_RET, 2 * D_CONV + 3 * D_RET
    for r in range(0, tm, RET_CHUNK):
        rows = pl.ds(r, RET_CHUNK)
        c2 = c2_ref[rows, :]
        s2 = s2_ref[rows, :]
        for hh in range(N_RET_HEADS):
            hc = hh * RET_HEAD_DIM
            qs = proj[rows, q0 + hc:q0 + hc + RET_HEAD_DIM]
            ks = proj[rows, k0 + hc:k0 + hc + RET_HEAD_DIM]
            vs = proj[rows, v0 + hc:v0 + hc + RET_HEAD_DIM]
            gs = proj[rows, g0 + hc:g0 + hc + RET_HEAD_DIM]
            qr = qs * c2 + pltpu.roll(qs, RET_HEAD_DIM // 2, 1) * s2
            kr = ks * c2 + pltpu.roll(ks, RET_HEAD_DIM // 2, 1) * s2
            qb = qr.astype(BF16)
            kb = kr.astype(BF16)
            scores = lax.dot_general(qb, kb, (((1,), (1,)), ((), ())), preferred_element_type=F32)
            scores = scores * din_ref[hh]
            inner = _dot(scores.astype(BF16), vs.astype(BF16))
            st = state[hh]
            cross = _dot((qr * xi_ref[hh]).astype(BF16), st.astype(BF16))
            kv = lax.dot_general(kb, (vs * zeta_ref[hh]).astype(BF16), (((0,), (0,)), ((), ())),
                                 preferred_element_type=F32)
            state[hh] = st * cdec_ref[hh] + kv
            o = inner + cross
            mu = jnp.mean(o, axis=-1, keepdims=True)
            oc = o - mu
            var = jnp.mean(oc * oc, axis=-1, keepdims=True)
            o = oc * lax.rsqrt(var + EPS) * gng_ref[:, hc:hc + RET_HEAD_DIM]
            mixin[rows, D_CONV + hc:D_CONV + hc + RET_HEAD_DIM] = (gs * jax.nn.sigmoid(gs) * o).astype(BF16)

    mix = mix_conv + _dot(mixin[:, D_CONV:D_MIX], wout[D_CONV:D_MIX, :])
    o_ref[...] = x_ref[...] + _rms(mix, gpost_ref[...])


def _layer_spec(a, layer):
    zeros = (0,) * (a.ndim - 1)
    return pl.BlockSpec((None,) + a.shape[1:], lambda b, t: (layer,) + zeros, pipeline_mode=pl.Buffered(1))


def _const_spec(a):
    zeros = (0,) * a.ndim
    return pl.BlockSpec(a.shape, lambda b, t: zeros, pipeline_mode=pl.Buffered(1))


def _mixer_call(layer, x, c2, s2, params, tables):
    bsz, seq, _ = x.shape
    tm = TM_MIX
    tile = lambda d: pl.BlockSpec((None, tm, d), lambda b, t: (b, t, 0))
    return pl.pallas_call(
        _mixer_kernel,
        grid=(bsz, seq // tm),
        in_specs=([tile(D_MODEL), tile(RET_HEAD_DIM), tile(RET_HEAD_DIM)]
                  + [_layer_spec(a, layer) for a in params] + [_const_spec(a) for a in tables]),
        out_specs=tile(D_MODEL),
        out_shape=jax.ShapeDtypeStruct(x.shape, x.dtype),
        scratch_shapes=[
            pltpu.VMEM((tm, D_IN_PROJ), F32),
            pltpu.VMEM((SUBLANES, CONV_HALO + tm, D_CONV), F32),
            pltpu.VMEM((tm, D_MIX), BF16),
            pltpu.VMEM((N_RET_HEADS, RET_HEAD_DIM, RET_HEAD_DIM), F32),
            pltpu.VMEM((D_MODEL, D_IN_PROJ), BF16),
            pltpu.VMEM((D_MIX, D_MODEL), BF16),
            pltpu.VMEM((tm, D_CONV), F32),
        ],
        compiler_params=pltpu.CompilerParams(
            dimension_semantics=("arbitrary", "arbitrary"), vmem_limit_bytes=VMEM_LIMIT_BYTES),
        name="mixer",
    )(x, c2, s2, *params, *tables)


def _ffn_kernel(x_ref, p_ref, gpre_ref, gpost_ref, wup_ref, fcw_ref, fcb_ref, wdown_ref, wgate_ref, wple_ref,
                o_ref, hbuf, upv, upg, act, hist, wup, wdown, wgate, wple):
    tm = x_ref.shape[0]

    @pl.when((pl.program_id(0) == 0) & (pl.program_id(1) == 0))
    def _():
        for c in range(0, 2 * D_FF, FFN_COLS):
            wup[:, c:c + FFN_COLS] = wup_ref[:, c:c + FFN_COLS]
        for c in range(0, D_FF, FFN_COLS):
            wdown[c:c + FFN_COLS, :] = wdown_ref[c:c + FFN_COLS, :]
        wgate[...] = wgate_ref[...]
        wple[...] = wple_ref[...]

    @pl.when(pl.program_id(1) == 0)
    def _():
        hist[...] = jnp.zeros_like(hist)

    hbuf[...] = _rms(x_ref[...], gpre_ref[...]).astype(BF16)

    for jn, j in enumerate(range(0, D_FF, FFN_COLS)):
        slot = jn % 2
        for ub, col in ((upv, j), (upg, D_FF + j)):
            cols = pl.ds(col, FFN_COLS)
            ub[slot, 0:FFN_HALO, :] = hist[:, cols]
            ub[slot, FFN_HALO:FFN_HALO + tm, :] = _dot(hbuf[...], wup[:, cols])
            hist[:, cols] = ub[slot, tm:tm + FFN_HALO, :]
        for r in range(0, tm, FFN_ROWS):
            conv = []
            for ub, col in ((upv, j), (upg, D_FF + j)):
                cols = pl.ds(col, FFN_COLS)
                a = ub[slot, pl.ds(r, FFN_ROWS + SUBLANES), :]
                s1 = pltpu.roll(a, 1, 0)[SUBLANES:]
                s2 = pltpu.roll(a, 2, 0)[SUBLANES:]
                conv.append(fcb_ref[:, cols] + fcw_ref[2:3, cols] * a[SUBLANES:]
                            + fcw_ref[1:2, cols] * s1 + fcw_ref[0:1, cols] * s2)
            val, gate = conv
            th = jnp.tanh(gate * (GELU_C0 + GELU_C1 * (gate * gate)))
            hg = 0.5 * gate
            act[pl.ds(r, FFN_ROWS), pl.ds(j, FFN_COLS)] = ((hg + hg * th) * val).astype(BF16)

    f = _dot(act[...], wdown[...])
    y = x_ref[...] + _rms(f, gpost_ref[...])
    gate = jax.nn.sigmoid(_dot(y.astype(BF16), wgate[...]))
    o_ref[...] = y + gate * _dot(p_ref[...].astype(BF16), wple[...])


def _ffn_call(layer, x, p, params):
    bsz, seq, _ = x.shape
    tm = TM_FFN
    tile = lambda d: pl.BlockSpec((None, tm, d), lambda b, t: (b, t, 0))
    p_tile = pl.BlockSpec((None, None, tm, PLE_DIM), lambda b, t: (layer, b, t, 0))
    up_block = pltpu.VMEM((2, FFN_HALO + tm, FFN_COLS), F32)
    return pl.pallas_call(
        _ffn_kernel,
        grid=(bsz, seq // tm),
        in_specs=[tile(D_MODEL), p_tile] + [_layer_spec(a, layer) for a in params],
        out_specs=tile(D_MODEL),
        out_shape=jax.ShapeDtypeStruct(x.shape, x.dtype),
        scratch_shapes=[
            pltpu.VMEM((tm, D_MODEL), BF16),
            up_block, up_block,
            pltpu.VMEM((tm, D_FF), BF16),
            pltpu.VMEM((FFN_HALO, 2 * D_FF), F32),
            pltpu.VMEM((D_MODEL, 2 * D_FF), BF16),
            pltpu.VMEM((D_FF, D_MODEL), BF16),
            pltpu.VMEM((D_MODEL, D_MODEL), BF16),
            pltpu.VMEM((PLE_DIM, D_MODEL), BF16),
        ],
        compiler_params=pltpu.CompilerParams(
            dimension_semantics=("arbitrary", "arbitrary"), vmem_limit_bytes=VMEM_LIMIT_BYTES),
        name="ffn_ple",
    )(x, p, *params)


def _retention_tables():
    n_h, c = N_RET_HEADS, RET_CHUNK
    log_g = jnp.log1p(-(2.0 ** (-5.0 - jnp.arange(n_h, dtype=F32))))
    idx = jnp.arange(c, dtype=F32)
    rel = idx[:, None] - idx[None, :]
    decay_in = jnp.where(rel >= 0, jnp.exp(log_g[:, None, None] * jnp.maximum(rel, 0.0)), 0.0)
    zeta = jnp.exp(log_g[:, None] * (c - 1 - idx))
    xi = jnp.exp(log_g[:, None] * (idx + 1))
    chunk_decay = jnp.exp(log_g * c)
    lanes = (n_h, c, RET_HEAD_DIM)
    key_scale = RET_HEAD_DIM ** -0.5
    decay_in = decay_in * key_scale
    zeta = zeta * key_scale
    return (decay_in, jnp.broadcast_to(zeta[:, :, None], lanes), jnp.broadcast_to(xi[:, :, None], lanes),
            jnp.broadcast_to(chunk_decay[:, None, None], (n_h, 1, RET_HEAD_DIM)))


def kernel(x, p, positions, norm_mix_pre, norm_mix_post, norm_ffn_pre, norm_ffn_post, w_in, conv_w, conv_b,
           conv_ln_g, conv_ln_b, ret_gn_g, w_out, ffn_up, ffn_conv_w, ffn_conv_b, ffn_down, ple_gate, ple_proj):
    bsz, seq, _ = x.shape
    depth = w_in.shape[0]
    c2, s2 = _rotary_tables(positions)
    c2 = c2.reshape(bsz, seq, RET_HEAD_DIM)
    s2 = s2.reshape(bsz, seq, RET_HEAD_DIM)
    tables = _retention_tables()
    row = lambda a: a[:, None, :]
    mixer_params = (row(norm_mix_pre), row(norm_mix_post), w_in, conv_w, row(conv_b),
                    row(conv_ln_g), row(conv_ln_b), row(ret_gn_g), w_out)
    ffn_params = (row(norm_ffn_pre), row(norm_ffn_post), ffn_up.astype(BF16), ffn_conv_w, row(ffn_conv_b),
                  ffn_down.astype(BF16), ple_gate.astype(BF16), ple_proj.astype(BF16))
    for layer in range(depth):
        x = _mixer_call(layer, x, c2, s2, mixer_params, tables)
        x = _ffn_call(layer, x, p, ffn_params)
    return x
```

```python
import jax
import jax.numpy as jnp
from jax import lax
from jax.experimental import pallas as pl
from jax.experimental.pallas import tpu as pltpu

D_MODEL = 1024
PLE_DIM = 256
D_CONV = 512
CONV_WIDTH = 31
N_RET_HEADS = 4
RET_HEAD_DIM = 128
D_RET = N_RET_HEADS * RET_HEAD_DIM
D_MIX = D_CONV + D_RET
D_IN_PROJ = 2 * D_CONV + 4 * D_RET
RET_CHUNK = 128
ROPE_BASE = 10000.0
D_FF = 2816
FFN_CONV_WIDTH = 3
EPS = 1e-6

LANES = 128
SUBLANES = 8
VMEM_LIMIT_BYTES = 56 * 1024 * 1024

TM_MIX = 512
TM_FFN = 512
TS_ROT = 2048
CONV_HALO = 32
CONV_ROWS = 64
FFN_COLS = 256
FFN_ROWS = 512
FFN_HALO = SUBLANES
GELU_C0 = 0.7978845608028654
GELU_C1 = 0.044715 * GELU_C0

F32 = jnp.float32
BF16 = jnp.bfloat16


def _rms(x, g):
    ms = jnp.mean(x * x, axis=-1, keepdims=True)
    return x * lax.rsqrt(ms + EPS) * g


def _dot(a, b):
    return jnp.dot(a, b, preferred_element_type=F32)


def _rot_kernel(pos_ref, invf_ref, c2_ref, s2_ref):
    ts = pos_ref.shape[0]
    h = ts // 2
    half = RET_HEAD_DIM // 2
    low = lax.broadcasted_iota(jnp.int32, (h, RET_HEAD_DIM), 1) < half
    pos = pos_ref[...].astype(F32)
    ang = jnp.where(low, pos[0:h], pos[h:ts]) * invf_ref[...]
    c = jnp.cos(ang)
    s = jnp.sin(ang)
    c_sw = pltpu.roll(c, half, 1)
    s_sw = pltpu.roll(s, half, 1)
    c2_ref[0:h, :] = jnp.where(low, c, c_sw)
    c2_ref[h:ts, :] = jnp.where(low, c_sw, c)
    s2_ref[0:h, :] = jnp.where(low, -s, s_sw)
    s2_ref[h:ts, :] = jnp.where(low, -s_sw, s)


def _rotary_tables(positions):
    n = positions.size
    half = RET_HEAD_DIM // 2
    inv_freq = ROPE_BASE ** (-jnp.arange(half, dtype=F32) / half)
    invf2 = jnp.concatenate([inv_freq, inv_freq])[None, :]
    row = pl.BlockSpec((1, RET_HEAD_DIM), lambda i: (0, 0))
    tab = pl.BlockSpec((TS_ROT, RET_HEAD_DIM), lambda i: (i, 0))
    return pl.pallas_call(
        _rot_kernel,
        grid=(n // TS_ROT,),
        in_specs=[pl.BlockSpec((TS_ROT, 1), lambda i: (i, 0)), row],
        out_specs=[tab, tab],
        out_shape=[jax.ShapeDtypeStruct((n, RET_HEAD_DIM), F32)] * 2,
        name="rotary_tables",
    )(positions.reshape(n, 1), invf2)


def _mixer_kernel(x_ref, c2_ref, s2_ref, gpre_ref, gpost_ref, win_ref, cw_ref, cb_ref, lng_ref, lnb_ref,
                  gng_ref, wout_ref, din_ref, zeta_ref, xi_ref, cdec_ref, o_ref,
                  proj, ush, mixin, state, win, wout, ybuf):
    tm = x_ref.shape[0]
    n_u = CONV_HALO + tm

    @pl.when(pl.program_id(1) == 0)
    def _():
        ush[0, 0:CONV_HALO, :] = jnp.zeros((CONV_HALO, D_CONV), F32)
        state[...] = jnp.zeros_like(state)

    @pl.when((pl.program_id(0) == 0) & (pl.program_id(1) == 0))
    def _():
        for c in range(0, D_CONV, LANES):
            win[:, 2 * c:2 * c + LANES] = win_ref[:, c:c + LANES].astype(BF16)
            win[:, 2 * c + LANES:2 * c + 2 * LANES] = win_ref[:, D_CONV + c:D_CONV + c + LANES].astype(BF16)
        for c in range(2 * D_CONV, D_IN_PROJ, 2 * D_CONV):
            win[:, c:c + 2 * D_CONV] = win_ref[:, c:c + 2 * D_CONV].astype(BF16)
        wout[...] = wout_ref[...].astype(BF16)

    h = _rms(x_ref[...], gpre_ref[...]).astype(BF16)
    proj[...] = _dot(h, win[...])

    first = CONV_HALO - (CONV_WIDTH - 1)
    for c in range(0, D_CONV, LANES):
        cols = pl.ds(c, LANES)
        for r in range(0, tm, CONV_ROWS):
            rows = pl.ds(r, CONV_ROWS)
            ush[0, pl.ds(CONV_HALO + r, CONV_ROWS), cols] = (
                proj[rows, 2 * c:2 * c + LANES] * jax.nn.sigmoid(proj[rows, 2 * c + LANES:2 * c + 2 * LANES]))
        u = ush[0, :, cols]
        for s in range(1, SUBLANES):
            ush[s, :, cols] = pltpu.roll(u, n_u - s, 0)
        for r in range(0, tm, CONV_ROWS):
            acc = jnp.broadcast_to(cb_ref[:, cols], (CONV_ROWS, LANES))
            for k in range(CONV_WIDTH):
                off = first + k
                acc = acc + cw_ref[k:k + 1, cols] * ush[off % SUBLANES,
                                                         pl.ds(r + off - off % SUBLANES, CONV_ROWS), cols]
            ybuf[pl.ds(r, CONV_ROWS), cols] = acc
    for r in range(0, tm, CONV_ROWS):
        y = ybuf[pl.ds(r, CONV_ROWS), :]
        mu = jnp.mean(y, axis=-1, keepdims=True)
        yc = y - mu
        var = jnp.mean(yc * yc, axis=-1, keepdims=True)
        y = yc * lax.rsqrt(var + EPS) * lng_ref[...] + lnb_ref[...]
        mixin[pl.ds(r, CONV_ROWS), 0:D_CONV] = (y * jax.nn.sigmoid(y)).astype(BF16)
    ush[0, 0:CONV_HALO, :] = ush[0, tm:tm + CONV_HALO, :]
    mix_conv = _dot(mixin[:, 0:D_CONV], wout[0:D_CONV, :])

    q0, k0, v0, g0 = 2 * D_CONV, 2 * D_CONV + D_RET, 2 * D_CONV + 2 * D_RET, 2 * D_CONV + 3 * D_RET
    for r in range(0, tm, RET_CHUNK):
        rows = pl.ds(r, RET_CHUNK)
        c2 = c2_ref[rows, :]
        s2 = s2_ref[rows, :]
        for hh in range(N_RET_HEADS):
            hc = hh * RET_HEAD_DIM
            qs = proj[rows, q0 + hc:q0 + hc + RET_HEAD_DIM]
            ks = proj[rows, k0 + hc:k0 + hc + RET_HEAD_DIM]
            vs = proj[rows, v0 + hc:v0 + hc + RET_HEAD_DIM]
            gs = proj[rows, g0 + hc:g0 + hc + RET_HEAD_DIM]
            qr = qs * c2 + pltpu.roll(qs, RET_HEAD_DIM // 2, 1) * s2
            kr = ks * c2 + pltpu.roll(ks, RET_HEAD_DIM // 2, 1) * s2
            qb = qr.astype(BF16)
            kb = kr.astype(BF16)
            scores = lax.dot_general(qb, kb, (((1,), (1,)), ((), ())), preferred_element_type=F32)
            scores = scores * din_ref[hh]
            inner = _dot(scores.astype(BF16), vs.astype(BF16))
            st = state[hh]
            cross = _dot((qr * xi_ref[hh]).astype(BF16), st.astype(BF16))
            kv = lax.dot_general(kb, (vs * zeta_ref[hh]).astype(BF16), (((0,), (0,)), ((), ())),
                                 preferred_element_type=F32)
            state[hh] = st * cdec_ref[hh] + kv
            o = inner + cross
            mu = jnp.mean(o, axis=-1, keepdims=True)
            oc = o - mu
            var = jnp.mean(oc * oc, axis=-1, keepdims=True)
            o = oc * lax.rsqrt(var + EPS) * gng_ref[:, hc:hc + RET_HEAD_DIM]
            mixin[rows, D_CONV + hc:D_CONV + hc + RET_HEAD_DIM] = (gs * jax.nn.sigmoid(gs) * o).astype(BF16)

    mix = mix_conv + _dot(mixin[:, D_CONV:D_MIX], wout[D_CONV:D_MIX, :])
    o_ref[...] = x_ref[...] + _rms(mix, gpost_ref[...])


def _layer_spec(a, layer):
    zeros = (0,) * (a.ndim - 1)
    return pl.BlockSpec((None,) + a.shape[1:], lambda b, t: (layer,) + zeros, pipeline_mode=pl.Buffered(1))


def _const_spec(a):
    zeros = (0,) * a.ndim
    return pl.BlockSpec(a.shape, lambda b, t: zeros, pipeline_mode=pl.Buffered(1))


def _mixer_call(layer, x, c2, s2, params, tables):
    bsz, seq, _ = x.shape
    tm = TM_MIX
    tile = lambda d: pl.BlockSpec((None, tm, d), lambda b, t: (b, t, 0))
    return pl.pallas_call(
        _mixer_kernel,
        grid=(bsz, seq // tm),
        in_specs=([tile(D_MODEL), tile(RET_HEAD_DIM), tile(RET_HEAD_DIM)]
                  + [_layer_spec(a, layer) for a in params] + [_const_spec(a) for a in tables]),
        out_specs=tile(D_MODEL),
        out_shape=jax.ShapeDtypeStruct(x.shape, x.dtype),
        scratch_shapes=[
            pltpu.VMEM((tm, D_IN_PROJ), F32),
            pltpu.VMEM((SUBLANES, CONV_HALO + tm, D_CONV), F32),
            pltpu.VMEM((tm, D_MIX), BF16),
            pltpu.VMEM((N_RET_HEADS, RET_HEAD_DIM, RET_HEAD_DIM), F32),
            pltpu.VMEM((D_MODEL, D_IN_PROJ), BF16),
            pltpu.VMEM((D_MIX, D_MODEL), BF16),
            pltpu.VMEM((tm, D_CONV), F32),
        ],
        compiler_params=pltpu.CompilerParams(
            dimension_semantics=("arbitrary", "arbitrary"), vmem_limit_bytes=VMEM_LIMIT_BYTES),
        name="mixer",
    )(x, c2, s2, *params, *tables)


def _ffn_kernel(x_ref, p_ref, gpre_ref, gpost_ref, wup_ref, fcw_ref, fcb_ref, wdown_ref, wgate_ref, wple_ref,
                o_ref, hbuf, upv, upg, act, hist, wup, wdown, wgate, wple):
    tm = x_ref.shape[0]

    @pl.when((pl.program_id(0) == 0) & (pl.program_id(1) == 0))
    def _():
        for c in range(0, 2 * D_FF, FFN_COLS):
            wup[:, c:c + FFN_COLS] = wup_ref[:, c:c + FFN_COLS]
        for c in range(0, D_FF, FFN_COLS):
            wdown[c:c + FFN_COLS, :] = wdown_ref[c:c + FFN_COLS, :]
        wgate[...] = wgate_ref[...].astype(BF16)
        wple[...] = wple_ref[...].astype(BF16)

    @pl.when(pl.program_id(1) == 0)
    def _():
        hist[...] = jnp.zeros_like(hist)

    hbuf[...] = _rms(x_ref[...], gpre_ref[...]).astype(BF16)

    for jn, j in enumerate(range(0, D_FF, FFN_COLS)):
        slot = jn % 2
        for ub, col in ((upv, j), (upg, D_FF + j)):
            cols = pl.ds(col, FFN_COLS)
            ub[slot, 0:FFN_HALO, :] = hist[:, cols]
            ub[slot, FFN_HALO:FFN_HALO + tm, :] = _dot(hbuf[...], wup[:, cols])
            hist[:, cols] = ub[slot, tm:tm + FFN_HALO, :]
        for r in range(0, tm, FFN_ROWS):
            conv = []
            for ub, col in ((upv, j), (upg, D_FF + j)):
                cols = pl.ds(col, FFN_COLS)
                a = ub[slot, pl.ds(r, FFN_ROWS + SUBLANES), :]
                s1 = pltpu.roll(a, 1, 0)[SUBLANES:]
                s2 = pltpu.roll(a, 2, 0)[SUBLANES:]
                conv.append(fcb_ref[:, cols] + fcw_ref[2:3, cols] * a[SUBLANES:]
                            + fcw_ref[1:2, cols] * s1 + fcw_ref[0:1, cols] * s2)
            val, gate = conv
            th = jnp.tanh(gate * (GELU_C0 + GELU_C1 * (gate * gate)))
            hg = 0.5 * gate
            act[pl.ds(r, FFN_ROWS), pl.ds(j, FFN_COLS)] = ((hg + hg * th) * val).astype(BF16)

    f = _dot(act[...], wdown[...])
    y = x_ref[...] + _rms(f, gpost_ref[...])
    gate = jax.nn.sigmoid(_dot(y.astype(BF16), wgate[...]))
    o_ref[...] = y + gate * _dot(p_ref[...].astype(BF16), wple[...])


def _ffn_call(layer, x, p, params):
    bsz, seq, _ = x.shape
    tm = TM_FFN
    tile = lambda d: pl.BlockSpec((None, tm, d), lambda b, t: (b, t, 0))
    p_tile = pl.BlockSpec((None, None, tm, PLE_DIM), lambda b, t: (layer, b, t, 0))
    up_block = pltpu.VMEM((2, FFN_HALO + tm, FFN_COLS), F32)
    return pl.pallas_call(
        _ffn_kernel,
        grid=(bsz, seq // tm),
        in_specs=[tile(D_MODEL), p_tile] + [_layer_spec(a, layer) for a in params],
        out_specs=tile(D_MODEL),
        out_shape=jax.ShapeDtypeStruct(x.shape, x.dtype),
        scratch_shapes=[
            pltpu.VMEM((tm, D_MODEL), BF16),
            up_block, up_block,
            pltpu.VMEM((tm, D_FF), BF16),
            pltpu.VMEM((FFN_HALO, 2 * D_FF), F32),
            pltpu.VMEM((D_MODEL, 2 * D_FF), BF16),
            pltpu.VMEM((D_FF, D_MODEL), BF16),
            pltpu.VMEM((D_MODEL, D_MODEL), BF16),
            pltpu.VMEM((PLE_DIM, D_MODEL), BF16),
        ],
        compiler_params=pltpu.CompilerParams(
            dimension_semantics=("arbitrary", "arbitrary"), vmem_limit_bytes=VMEM_LIMIT_BYTES),
        name="ffn_ple",
    )(x, p, *params)


def _retention_tables():
    n_h, c = N_RET_HEADS, RET_CHUNK
    log_g = jnp.log1p(-(2.0 ** (-5.0 - jnp.arange(n_h, dtype=F32))))
    idx = jnp.arange(c, dtype=F32)
    rel = idx[:, None] - idx[None, :]
    decay_in = jnp.where(rel >= 0, jnp.exp(log_g[:, None, None] * jnp.maximum(rel, 0.0)), 0.0)
    zeta = jnp.exp(log_g[:, None] * (c - 1 - idx))
    xi = jnp.exp(log_g[:, None] * (idx + 1))
    chunk_decay = jnp.exp(log_g * c)
    lanes = (n_h, c, RET_HEAD_DIM)
    key_scale = RET_HEAD_DIM ** -0.5
    decay_in = decay_in * key_scale
    zeta = zeta * key_scale
    return (decay_in, jnp.broadcast_to(zeta[:, :, None], lanes), jnp.broadcast_to(xi[:, :, None], lanes),
            jnp.broadcast_to(chunk_decay[:, None, None], (n_h, 1, RET_HEAD_DIM)))


def kernel(x, p, positions, norm_mix_pre, norm_mix_post, norm_ffn_pre, norm_ffn_post, w_in, conv_w, conv_b,
           conv_ln_g, conv_ln_b, ret_gn_g, w_out, ffn_up, ffn_conv_w, ffn_conv_b, ffn_down, ple_gate, ple_proj):
    bsz, seq, _ = x.shape
    depth = w_in.shape[0]
    c2, s2 = _rotary_tables(positions)
    c2 = c2.reshape(bsz, seq, RET_HEAD_DIM)
    s2 = s2.reshape(bsz, seq, RET_HEAD_DIM)
    tables = _retention_tables()
    row = lambda a: a[:, None, :]
    mixer_params = (row(norm_mix_pre), row(norm_mix_post), w_in, conv_w, row(conv_b),
                    row(conv_ln_g), row(conv_ln_b), row(ret_gn_g), w_out)
    ffn_params = (row(norm_ffn_pre), row(norm_ffn_post), ffn_up.astype(BF16), ffn_conv_w, row(ffn_conv_b),
                  ffn_down.astype(BF16), ple_gate, ple_proj)
    for layer in range(depth):
        x = _mixer_call(layer, x, c2, s2, mixer_params, tables)
        x = _ffn_call(layer, x, p, ffn_params)
    return x
```
